```python
import math, functools
import jax, jax.numpy as jnp
from jax import lax
import numpy as np

D_MODEL = 1024
BATCH = 8
SEQ = 8192
DEPTH = 1
DEC_BATCH = 128
DEC_SEQ = 4
PAST_LEN = 8192
PAGE_SIZE = 128

FOX_HEADS = 8
FOX_HEAD_DIM = 64
FOX_WIDTH = FOX_HEADS * FOX_HEAD_DIM
FOX_SCALE = FOX_HEAD_DIM ** -0.5
Q_BLOCK = 128
FORGET_BIAS = 2.0
S5_GROUP = 16
S5_WIDTH = 512
S5_GROUPS = S5_WIDTH // S5_GROUP
S5_STATE = 64
DT_MIN = 1e-3
DT_MAX = 1e-1
A_RE_MAX = -1e-4
N_MEM = 256
MEM_HEADS = 4
MEM_HEAD_DIM = 128
MEM_WIDTH = MEM_HEADS * MEM_HEAD_DIM
MEM_SCALE = MEM_HEAD_DIM ** -0.5
N_BRANCH = 3
PROJ_WIDTH = 3 * FOX_WIDTH + FOX_HEADS + S5_WIDTH + MEM_WIDTH + N_BRANCH * D_MODEL
N_EXPERT_GROUPS = 4
EXPERTS_PER_GROUP = 4
N_EXPERTS = N_EXPERT_GROUPS * EXPERTS_PER_GROUP
D_EXPERT = 256
TOP_K_WITHIN = 2

EPS = 1e-6
NEG_INF = -1e30

kernel_name = "hybrid_fox_s5_memory_hiermoe_step"


def _rmsnorm(x, g):
    xf = x.astype(jnp.float32)
    y = xf * lax.rsqrt(jnp.mean(xf * xf, axis=-1, keepdims=True) + EPS)
    return (y * g.astype(jnp.float32)).astype(x.dtype)


def _split_projection(z, b_forget):
    b, t = z.shape[0], z.shape[1]
    o1 = FOX_WIDTH
    o2 = 2 * FOX_WIDTH
    o3 = 3 * FOX_WIDTH
    o4 = o3 + FOX_HEADS
    o5 = o4 + S5_WIDTH
    o6 = o5 + MEM_WIDTH
    q = z[..., :o1].reshape(b, t, FOX_HEADS, FOX_HEAD_DIM)
    k = z[..., o1:o2].reshape(b, t, FOX_HEADS, FOX_HEAD_DIM)
    v = z[..., o2:o3].reshape(b, t, FOX_HEADS, FOX_HEAD_DIM)
    logf = jax.nn.log_sigmoid(z[..., o3:o4].astype(jnp.float32) + b_forget.astype(jnp.float32))
    u = z[..., o4:o5]
    qm = z[..., o5:o6].reshape(b, t, MEM_HEADS, MEM_HEAD_DIM)
    gates = jax.nn.sigmoid(z[..., o6:].reshape(b, t, N_BRANCH, D_MODEL))
    return q, k, v, logf, u, qm, gates


def _fox_prompt(q, k, v, logf):
    b, s = q.shape[0], q.shape[1]
    nb = s // Q_BLOCK
    c = jnp.cumsum(logf, axis=1).transpose(0, 2, 1)
    qb = (q * FOX_SCALE).reshape(b, nb, Q_BLOCK, FOX_HEADS, FOX_HEAD_DIM).transpose(1, 0, 2, 3, 4)
    cb = c.reshape(b, FOX_HEADS, nb, Q_BLOCK).transpose(2, 0, 1, 3)
    qpos = jnp.arange(s).reshape(nb, Q_BLOCK)
    kpos = jnp.arange(s)

    def one_block(args):
        q_i, c_i, p_i = args
        sc = jnp.einsum('bqhd,bkhd->bhqk', q_i, k).astype(jnp.float32)
        sc = sc + c_i[..., :, None] - c[:, :, None, :]
        sc = jnp.where(kpos[None, :] <= p_i[:, None], sc, NEG_INF)
        p = jax.nn.softmax(sc, axis=-1).astype(v.dtype)
        return jnp.einsum('bhqk,bkhd->bqhd', p, v)

    o = lax.map(one_block, (qb, cb, qpos))
    return o.transpose(1, 0, 2, 3, 4).reshape(b, s, FOX_WIDTH)


def _fox_sample(q, k, v, logf, past_k, past_v, past_logf):
    b, t = q.shape[0], q.shape[1]
    n_past = past_k.shape[1]
    c = jnp.cumsum(jnp.concatenate([past_logf.astype(jnp.float32), logf], axis=1), axis=1)
    c = c.transpose(0, 2, 1)
    c_q = c[:, :, n_past:, None]
    qs = q * FOX_SCALE
    sc_past = jnp.einsum('bqhd,bkhd->bhqk', qs, past_k.astype(q.dtype)).astype(jnp.float32)
    sc_past = sc_past + c_q - c[:, :, None, :n_past]
    sc_new = jnp.einsum('bqhd,bkhd->bhqk', qs, k).astype(jnp.float32)
    sc_new = sc_new + c_q - c[:, :, None, n_past:]
    causal = jnp.arange(t)[None, :] <= jnp.arange(t)[:, None]
    sc_new = jnp.where(causal, sc_new, NEG_INF)
    p = jax.nn.softmax(jnp.concatenate([sc_past, sc_new], axis=-1), axis=-1).astype(v.dtype)
    o = (jnp.einsum('bhqk,bkhd->bqhd', p[..., :n_past], past_v.astype(v.dtype))
         + jnp.einsum('bhqk,bkhd->bqhd', p[..., n_past:], v))
    return o.reshape(b, t, FOX_WIDTH)


def _s5_discretise(a_re, a_im, log_dt, b_re, b_im):
    a_re = jnp.minimum(a_re.astype(jnp.float32), A_RE_MAX)
    a_im = a_im.astype(jnp.float32)
    dt = jnp.exp(log_dt.astype(jnp.float32))[:, None]
    mag = jnp.exp(dt * a_re)
    ang = dt * a_im
    ab_re = mag * jnp.cos(ang)
    ab_im = mag * jnp.sin(ang)
    den = a_re * a_re + a_im * a_im
    n_re = ab_re - 1.0
    n_im = ab_im
    s_re = (n_re * a_re + n_im * a_im) / den
    s_im = (n_im * a_re - n_re * a_im) / den
    b_re = b_re.astype(jnp.float32)
    b_im = b_im.astype(jnp.float32)
    bb_re = s_re[..., None] * b_re - s_im[..., None] * b_im
    bb_im = s_re[..., None] * b_im + s_im[..., None] * b_re
    return ab_re, ab_im, bb_re, bb_im


def _ssm_combine(e1, e2):
    a1r, a1i, b1r, b1i = e1
    a2r, a2i, b2r, b2i = e2
    return (a2r * a1r - a2i * a1i,
            a2r * a1i + a2i * a1r,
            a2r * b1r - a2i * b1i + b2r,
            a2r * b1i + a2i * b1r + b2i)


def _s5_branch(u, h0_re, h0_im, lp):
    b, t = u.shape[0], u.shape[1]
    uf = u.astype(jnp.float32).reshape(b, t, S5_GROUPS, S5_GROUP)
    ab_re, ab_im, bb_re, bb_im = _s5_discretise(lp['s5_a_re'], lp['s5_a_im'], lp['s5_log_dt'],
                                                lp['s5_b_re'], lp['s5_b_im'])
    bu_re = jnp.einsum('btgp,gnp->btgn', uf, bb_re)
    bu_im = jnp.einsum('btgp,gnp->btgn', uf, bb_im)
    h0_re = h0_re.astype(jnp.float32)
    h0_im = h0_im.astype(jnp.float32)
    bu_re = bu_re.at[:, 0].add(ab_re * h0_re - ab_im * h0_im)
    bu_im = bu_im.at[:, 0].add(ab_re * h0_im + ab_im * h0_re)
    a_re_t = jnp.broadcast_to(ab_re, bu_re.shape)
    a_im_t = jnp.broadcast_to(ab_im, bu_im.shape)
    _, _, h_re, h_im = lax.associative_scan(_ssm_combine, (a_re_t, a_im_t, bu_re, bu_im), axis=1)
    y = (jnp.einsum('btgn,gpn->btgp', h_re, lp['s5_c_re'].astype(jnp.float32))
         - jnp.einsum('btgn,gpn->btgp', h_im, lp['s5_c_im'].astype(jnp.float32))
         + lp['s5_d'].astype(jnp.float32).reshape(S5_GROUPS, S5_GROUP) * uf)
    zg = jax.nn.gelu(y.reshape(b, t, S5_WIDTH))
    out = zg * jax.nn.sigmoid(zg @ lp['s5_w_glu'].astype(jnp.float32) + lp['s5_b_glu'].astype(jnp.float32))
    return out.astype(u.dtype), h_re[:, -1], h_im[:, -1]


def _memory_kv(mem, g_mem, w_mem_kv):
    b, m = mem.shape[0], mem.shape[1]
    kv = _rmsnorm(mem, g_mem) @ w_mem_kv
    k = kv[..., :MEM_WIDTH].reshape(b, m, MEM_HEADS, MEM_HEAD_DIM)
    v = kv[..., MEM_WIDTH:].reshape(b, m, MEM_HEADS, MEM_HEAD_DIM)
    return k, v


def _memory_attend(qm, mk, mv):
    b, t = qm.shape[0], qm.shape[1]
    sc = jnp.einsum('bthd,bmhd->bhtm', qm * MEM_SCALE, mk.astype(qm.dtype)).astype(jnp.float32)
    p = jax.nn.softmax(sc, axis=-1).astype(qm.dtype)
    return jnp.einsum('bhtm,bmhd->bthd', p, mv.astype(qm.dtype)).reshape(b, t, MEM_WIDTH)


def _hier_moe(h, lp):
    b, t = h.shape[0], h.shape[1]
    lg = (h @ lp['w_router_group']).astype(jnp.float32) + lp['b_router_group'].astype(jnp.float32)
    pg = jax.nn.softmax(lg, axis=-1)
    oh_g = jax.nn.one_hot(jnp.argmax(lg, axis=-1), N_EXPERT_GROUPS, dtype=jnp.float32)
    p_sel = jnp.sum(pg * oh_g, axis=-1)
    le = (jnp.einsum('btd,dge->btge', h, lp['w_router_expert']).astype(jnp.float32)
          + lp['b_router_expert'].astype(jnp.float32))
    le_sel = jnp.sum(le * oh_g[..., None], axis=-2)
    top_v, top_i = lax.top_k(le_sel, TOP_K_WITHIN)
    w_top = jax.nn.softmax(top_v, axis=-1)
    w_within = jnp.sum(jax.nn.one_hot(top_i, EXPERTS_PER_GROUP, dtype=jnp.float32) * w_top[..., None], axis=-2)
    gate = (p_sel[..., None, None] * oh_g[..., :, None] * w_within[..., None, :]).reshape(b, t, N_EXPERTS)
    hg = h @ lp['w_exp_gate']
    hu = h @ lp['w_exp_up']
    act = (jax.nn.silu(hg) * hu).reshape(b, t, N_EXPERTS, D_EXPERT) * gate[..., None].astype(hg.dtype)
    return act.reshape(b, t, N_EXPERTS * D_EXPERT) @ lp['w_exp_down']


def _decoder_layer(x, fox_attend, h0_re, h0_im, mem_k, mem_v, lp):
    z = _rmsnorm(x, lp['g_mix']) @ lp['w_in']
    q, k, v, logf, u, qm, gates = _split_projection(z, lp['b_forget'])
    o_fox = fox_attend(q, k, v, logf) @ lp['w_br_fox']
    o_s5, h_re, h_im = _s5_branch(u, h0_re, h0_im, lp)
    o_s5 = o_s5 @ lp['w_br_s5']
    o_mem = _memory_attend(qm, mem_k, mem_v) @ lp['w_br_mem']
    merged = gates[:, :, 0] * o_fox + gates[:, :, 1] * o_s5 + gates[:, :, 2] * o_mem
    x = x + merged @ lp['w_out']
    x = x + _hier_moe(_rmsnorm(x, lp['g_ffn']), lp)
    return x, k, v, logf, h_re, h_im


def setup_inputs(seed: int = 0) -> dict:
    key = jax.random.key(seed)
    ks = iter(jax.random.split(key, 48))

    def nrm(shape, scale):
        return scale * jax.random.normal(next(ks), shape, jnp.float32)

    n_pages = PAST_LEN // PAGE_SIZE
    n_used = DEC_BATCH * n_pages
    n_pool = (5 * n_used) // 4
    d = D_MODEL
    inp = {}
    inp['x_prompt'] = nrm((BATCH, SEQ, d), 1.0)
    inp['x_sample'] = nrm((DEC_BATCH, DEC_SEQ, d), 1.0)
    inp['cache_fox_k'] = nrm((DEPTH, n_pool, PAGE_SIZE, FOX_HEADS, FOX_HEAD_DIM), 1.0)
    inp['cache_fox_v'] = nrm((DEPTH, n_pool, PAGE_SIZE, FOX_HEADS, FOX_HEAD_DIM), 1.0)
    inp['cache_fox_logf'] = jax.nn.log_sigmoid(FORGET_BIAS + nrm((DEPTH, n_pool, PAGE_SIZE, FOX_HEADS), 1.0))
    inp['cache_mem_k'] = nrm((DEPTH, DEC_BATCH, N_MEM, MEM_HEADS, MEM_HEAD_DIM), 1.0)
    inp['cache_mem_v'] = nrm((DEPTH, DEC_BATCH, N_MEM, MEM_HEADS, MEM_HEAD_DIM), 1.0)
    inp['state_s5_re'] = nrm((DEPTH, DEC_BATCH, S5_GROUPS, S5_STATE), 0.1)
    inp['state_s5_im'] = nrm((DEPTH, DEC_BATCH, S5_GROUPS, S5_STATE), 0.1)
    perm = jax.random.permutation(next(ks), n_pool)
    inp['page_table'] = perm[:n_used].reshape(DEC_BATCH, n_pages).astype(jnp.int32)
    inp['mem_prompt'] = nrm((BATCH, N_MEM, d), 1.0)
    inp['g_mix'] = 1.0 + nrm((DEPTH, d), 0.01)
    inp['w_in'] = nrm((DEPTH, d, PROJ_WIDTH), d ** -0.5)
    inp['b_forget'] = FORGET_BIAS + nrm((DEPTH, FOX_HEADS), 0.1)
    inp['s5_a_re'] = -0.5 + nrm((DEPTH, S5_GROUPS, S5_STATE), 0.01)
    inp['s5_a_im'] = jnp.pi * jnp.arange(S5_STATE, dtype=jnp.float32) + nrm((DEPTH, S5_GROUPS, S5_STATE), 0.01)
    inp['s5_log_dt'] = jax.random.uniform(next(ks), (DEPTH, S5_GROUPS), jnp.float32,
                                          minval=math.log(DT_MIN), maxval=math.log(DT_MAX))
    inp['s5_b_re'] = nrm((DEPTH, S5_GROUPS, S5_STATE, S5_GROUP), (2 * S5_GROUP) ** -0.5)
    inp['s5_b_im'] = nrm((DEPTH, S5_GROUPS, S5_STATE, S5_GROUP), (2 * S5_GROUP) ** -0.5)
    inp['s5_c_re'] = nrm((DEPTH, S5_GROUPS, S5_GROUP, S5_STATE), 0.7)
    inp['s5_c_im'] = nrm((DEPTH, S5_GROUPS, S5_GROUP, S5_STATE), 0.7)
    inp['s5_d'] = nrm((DEPTH, S5_WIDTH), 0.5)
    inp['s5_w_glu'] = nrm((DEPTH, S5_WIDTH, S5_WIDTH), S5_WIDTH ** -0.5)
    inp['s5_b_glu'] = nrm((DEPTH, S5_WIDTH), 0.01)
    inp['g_mem'] = 1.0 + nrm((DEPTH, d), 0.01)
    inp['w_mem_kv'] = nrm((DEPTH, d, 2 * MEM_WIDTH), d ** -0.5)
    inp['w_br_fox'] = nrm((DEPTH, FOX_WIDTH, d), FOX_WIDTH ** -0.5)
    inp['w_br_s5'] = nrm((DEPTH, S5_WIDTH, d), S5_WIDTH ** -0.5)
    inp['w_br_mem'] = nrm((DEPTH, MEM_WIDTH, d), MEM_WIDTH ** -0.5)
    inp['w_out'] = nrm((DEPTH, d, d), d ** -0.5)
    inp['g_ffn'] = 1.0 + nrm((DEPTH, d), 0.01)
    inp['w_router_group'] = nrm((DEPTH, d, N_EXPERT_GROUPS), d ** -0.5)
    inp['b_router_group'] = nrm((DEPTH, N_EXPERT_GROUPS), 0.01)
    inp['w_router_expert'] = nrm((DEPTH, d, N_EXPERT_GROUPS, EXPERTS_PER_GROUP), d ** -0.5)
    inp['b_router_expert'] = nrm((DEPTH, N_EXPERT_GROUPS, EXPERTS_PER_GROUP), 0.01)
    inp['w_exp_gate'] = nrm((DEPTH, d, N_EXPERTS * D_EXPERT), d ** -0.5)
    inp['w_exp_up'] = nrm((DEPTH, d, N_EXPERTS * D_EXPERT), d ** -0.5)
    inp['w_exp_down'] = nrm((DEPTH, N_EXPERTS * D_EXPERT, d), D_EXPERT ** -0.5)
    inp['g_final'] = 1.0 + nrm((d,), 0.01)
    return inp


def reference(x_prompt, x_sample, cache_fox_k, cache_fox_v, cache_fox_logf, cache_mem_k, cache_mem_v,
              state_s5_re, state_s5_im, page_table, mem_prompt,
              g_mix, w_in, b_forget, s5_a_re, s5_a_im, s5_log_dt, s5_b_re, s5_b_im, s5_c_re, s5_c_im,
              s5_d, s5_w_glu, s5_b_glu, g_mem, w_mem_kv, w_br_fox, w_br_s5, w_br_mem, w_out, g_ffn,
              w_router_group, b_router_group, w_router_expert, b_router_expert,
              w_exp_gate, w_exp_up, w_exp_down, g_final):
    xp, xs = x_prompt, x_sample
    bp, db = xp.shape[0], xs.shape[0]
    fk_p, fv_p, fl_p, fk_s, fv_s, fl_s = [], [], [], [], [], []
    mk_list, mv_list = [], []
    sre_p, sim_p, sre_s, sim_s = [], [], [], []
    for l in range(DEPTH):
        lp = dict(g_mix=g_mix[l], w_in=w_in[l], b_forget=b_forget[l],
                  s5_a_re=s5_a_re[l], s5_a_im=s5_a_im[l], s5_log_dt=s5_log_dt[l],
                  s5_b_re=s5_b_re[l], s5_b_im=s5_b_im[l], s5_c_re=s5_c_re[l], s5_c_im=s5_c_im[l],
                  s5_d=s5_d[l], s5_w_glu=s5_w_glu[l], s5_b_glu=s5_b_glu[l],
                  w_br_fox=w_br_fox[l], w_br_s5=w_br_s5[l], w_br_mem=w_br_mem[l], w_out=w_out[l],
                  g_ffn=g_ffn[l], w_router_group=w_router_group[l], b_router_group=b_router_group[l],
                  w_router_expert=w_router_expert[l], b_router_expert=b_router_expert[l],
                  w_exp_gate=w_exp_gate[l], w_exp_up=w_exp_up[l], w_exp_down=w_exp_down[l])
        mk_p, mv_p = _memory_kv(mem_prompt, g_mem[l], w_mem_kv[l])
        zero_state = jnp.zeros((bp, S5_GROUPS, S5_STATE), jnp.float32)
        xp, kp, vp, lfp, hr_p, hi_p = _decoder_layer(xp, _fox_prompt, zero_state, zero_state, mk_p, mv_p, lp)
        past_k = cache_fox_k[l][page_table].reshape(db, -1, FOX_HEADS, FOX_HEAD_DIM)
        past_v = cache_fox_v[l][page_table].reshape(db, -1, FOX_HEADS, FOX_HEAD_DIM)
        past_lf = cache_fox_logf[l][page_table].reshape(db, -1, FOX_HEADS)
        fox_s = functools.partial(_fox_sample, past_k=past_k, past_v=past_v, past_logf=past_lf)
        xs, ks_, vs_, lfs, hr_s, hi_s = _decoder_layer(xs, fox_s, state_s5_re[l], state_s5_im[l],
                                                      cache_mem_k[l], cache_mem_v[l], lp)
        fk_p.append(kp); fv_p.append(vp); fl_p.append(lfp)
        fk_s.append(ks_); fv_s.append(vs_); fl_s.append(lfs)
        mk_list.append(mk_p); mv_list.append(mv_p)
        sre_p.append(hr_p); sim_p.append(hi_p); sre_s.append(hr_s); sim_s.append(hi_s)
    y_prompt = _rmsnorm(xp, g_final)
    y_sample = _rmsnorm(xs, g_final)
    return (y_prompt, y_sample,
            jnp.stack(fk_p), jnp.stack(fv_p), jnp.stack(fl_p),
            jnp.stack(fk_s), jnp.stack(fv_s), jnp.stack(fl_s),
            jnp.stack(mk_list), jnp.stack(mv_list),
            jnp.stack(sre_p), jnp.stack(sim_p), jnp.stack(sre_s), jnp.stack(sim_s))
```

```python
import functools
import math

import jax
import jax.numpy as jnp
from jax import lax
from jax.experimental import pallas as pl
from jax.experimental.pallas import tpu as pltpu

F32 = jnp.float32
BF16 = jnp.bfloat16

FOX_HEADS = 8
FOX_HEAD_DIM = 64
FOX_WIDTH = FOX_HEADS * FOX_HEAD_DIM
FOX_SCALE = FOX_HEAD_DIM ** -0.5
S5_GROUP = 16
S5_WIDTH = 512
S5_GROUPS = S5_WIDTH // S5_GROUP
S5_STATE = 64
S5_NSTATE = S5_GROUPS * S5_STATE
A_RE_MAX = -1e-4
MEM_HEADS = 4
MEM_HEAD_DIM = 128
MEM_WIDTH = MEM_HEADS * MEM_HEAD_DIM
MEM_SCALE = MEM_HEAD_DIM ** -0.5
N_BRANCH = 3
N_EXPERT_GROUPS = 4
EXPERTS_PER_GROUP = 4
D_EXPERT = 256
EPS = 1e-6
NEG_INF = -1e30

LANES = 128
SUBLANES = 8
MXU_DIM = 256
VMEM_LIMIT = 56 * 1024 * 1024

PROJ_TM = 512
CUMSUM_BLK = 512
FOX_TQ = 512
FOX_PAGES_PER_CHUNK = 8
S5_TT = 64
MEMATTN_TQ = 1024
MERGE_TM = 512
MOE_TM = 512


def _cparams(sem):
    return pltpu.CompilerParams(dimension_semantics=sem, vmem_limit_bytes=VMEM_LIMIT)


def _rms(x, g):
    return x * lax.rsqrt(jnp.mean(x * x, axis=-1, keepdims=True) + EPS) * g


def _dot(a, b):
    return jnp.dot(a, b, preferred_element_type=F32)


def _dot_nt(a, b):
    return lax.dot_general(a, b, (((1,), (1,)), ((), ())), preferred_element_type=F32)


def _split3(x):
    hi = x.astype(BF16)
    r = x - hi.astype(F32)
    mid = r.astype(BF16)
    lo = (r - mid.astype(F32)).astype(BF16)
    return hi, mid, lo


def _sigmoid(x):
    return 1.0 / (1.0 + jnp.exp(-x))


def _proj_kernel(x_ref, g_ref, w_ref, wft_ref, bf_ref,
                 q_ref, k32_ref, v32_ref, kb_ref, vb_ref, u_ref, qm_ref, lf_ref):
    xn = _rms(x_ref[...], g_ref[...]).astype(BF16)
    w = FOX_WIDTH
    q_ref[...] = (_dot(xn, w_ref[:, 0:w]) * FOX_SCALE).astype(BF16)
    k = _dot(xn, w_ref[:, w:2 * w])
    k32_ref[...] = k
    kb_ref[...] = k.astype(BF16)
    v = _dot(xn, w_ref[:, 2 * w:3 * w])
    v32_ref[...] = v
    vb_ref[...] = v.astype(BF16)
    u_ref[...] = _dot(xn, w_ref[:, 3 * w:3 * w + S5_WIDTH])
    o = 3 * w + S5_WIDTH
    qm_ref[...] = (_dot(xn, w_ref[:, o:o + MEM_WIDTH]) * MEM_SCALE).astype(BF16)
    zf = _dot_nt(wft_ref[...], xn)[0:FOX_HEADS, :] + bf_ref[...]
    lf_ref[...] = jnp.minimum(zf, 0.0) - jnp.log1p(jnp.exp(-jnp.abs(zf)))


def _proj(x2d, g_mix, w_a, wft, b_forget):
    n, d = x2d.shape
    tm = min(PROJ_TM, n)
    tok = lambda width: pl.BlockSpec((tm, width), lambda i: (i, 0))
    full = lambda a: pl.BlockSpec(a.shape, lambda i: (0,) * a.ndim)
    out_shape = (
        jax.ShapeDtypeStruct((n, FOX_WIDTH), BF16),
        jax.ShapeDtypeStruct((n, FOX_WIDTH), F32),
        jax.ShapeDtypeStruct((n, FOX_WIDTH), F32),
        jax.ShapeDtypeStruct((n, FOX_WIDTH), BF16),
        jax.ShapeDtypeStruct((n, FOX_WIDTH), BF16),
        jax.ShapeDtypeStruct((n, S5_WIDTH), F32),
        jax.ShapeDtypeStruct((n, MEM_WIDTH), BF16),
        jax.ShapeDtypeStruct((FOX_HEADS, n), F32),
    )
    g2 = g_mix.reshape(1, d)
    bf2 = b_forget.reshape(FOX_HEADS, 1)
    return pl.pallas_call(
        _proj_kernel,
        grid=(n // tm,),
        in_specs=[tok(d), full(g2), full(w_a), full(wft), full(bf2)],
        out_specs=(tok(FOX_WIDTH), tok(FOX_WIDTH), tok(FOX_WIDTH), tok(FOX_WIDTH), tok(FOX_WIDTH),
                   tok(S5_WIDTH), tok(MEM_WIDTH), pl.BlockSpec((FOX_HEADS, tm), lambda i: (0, i))),
        out_shape=out_shape,
        compiler_params=_cparams(("parallel",)),
        name="proj",
    )(x2d, g2, w_a, wft, bf2)


def _tri_upper(n):
    r = lax.broadcasted_iota(jnp.int32, (n, n), 0)
    c = lax.broadcasted_iota(jnp.int32, (n, n), 1)
    return jnp.where(r <= c, 1.0, 0.0).astype(BF16)


def _cumsum_lanes(x, tri):
    hi, mid, lo = _split3(x)
    return _dot(hi, tri) + _dot(mid, tri) + _dot(lo, tri)


def _cumsum_kernel(lf_ref, c_ref, *, blk):
    t = lf_ref.shape[1]
    tri = _tri_upper(blk)

    def body(j, carry):
        off = pl.multiple_of(j * blk, blk)
        w = _cumsum_lanes(lf_ref[:, pl.ds(off, blk)], tri) + carry
        c_ref[:, pl.ds(off, blk)] = w
        return w[:, blk - 1:blk]

    lax.fori_loop(0, t // blk, body, jnp.zeros((FOX_HEADS, 1), F32))


def _cumsum(lft, nb, t):
    blk = min(CUMSUM_BLK, t)
    spec = pl.BlockSpec((FOX_HEADS, t), lambda b: (0, b))
    return pl.pallas_call(
        functools.partial(_cumsum_kernel, blk=blk),
        grid=(nb,),
        in_specs=[spec],
        out_specs=spec,
        out_shape=jax.ShapeDtypeStruct(lft.shape, F32),
        compiler_params=_cparams(("parallel",)),
        name="logf_cumsum",
    )(lft)


HEADS_PER_SLAB = MXU_DIM // FOX_HEAD_DIM


def _fox_kernel(q_ref, k_ref, v_ref, c_ref, o_ref, m_sc, l_sc, acc_sc, *, tq):
    qi = pl.program_id(2)
    q = q_ref[0]
    lane = lax.broadcasted_iota(jnp.int32, (1, MXU_DIM), 1)
    q0 = pl.multiple_of(qi * tq, tq)
    row = lax.broadcasted_iota(jnp.int32, (tq, tq), 0)
    col = lax.broadcasted_iota(jnp.int32, (tq, tq), 1)
    out = jnp.zeros((tq, MXU_DIM), F32)
    for h in range(HEADS_PER_SLAB):
        hmask = (lane // FOX_HEAD_DIM) == h
        qh = jnp.where(hmask, q, jnp.zeros_like(q))
        cref = c_ref[0, h:h + 1, pl.ds(q0, LANES)][:, 0:1]
        m_sc[...] = jnp.full(m_sc.shape, NEG_INF, F32)
        l_sc[...] = jnp.zeros(l_sc.shape, F32)
        acc_sc[...] = jnp.zeros(acc_sc.shape, F32)

        def step(j, masked):
            off = pl.multiple_of(j * tq, tq)
            kj = k_ref[0, pl.ds(off, tq), :]
            vj = v_ref[0, pl.ds(off, tq), :]
            s = _dot_nt(qh, kj) + (cref - c_ref[0, h:h + 1, pl.ds(off, tq)])
            if masked:
                s = jnp.where(col <= row, s, NEG_INF)
            m_old = m_sc[...]
            m_new = jnp.maximum(m_old, jnp.max(s, axis=-1, keepdims=True))
            alpha = jnp.exp(m_old - m_new)
            p = jnp.exp(s - m_new)
            l_sc[...] = alpha * l_sc[...] + jnp.sum(p, axis=-1, keepdims=True)
            acc_sc[...] = alpha * acc_sc[...] + _dot(p.astype(BF16), vj)
            m_sc[...] = m_new

        def body(j, carry):
            step(j, False)
            return carry

        lax.fori_loop(0, qi, body, 0)
        step(qi, True)
        out = jnp.where(hmask, acc_sc[...] / l_sc[...], out)
    o_ref[0] = out.astype(o_ref.dtype)


def _fox_prompt(q, k, v, c, nb, t):
    tq = min(FOX_TQ, t)
    nslab = FOX_WIDTH // MXU_DIM
    return pl.pallas_call(
        functools.partial(_fox_kernel, tq=tq),
        grid=(nb, nslab, t // tq),
        in_specs=[
            pl.BlockSpec((1, tq, MXU_DIM), lambda b, g, i: (b, i, g)),
            pl.BlockSpec((1, t, MXU_DIM), lambda b, g, i: (b, 0, g)),
            pl.BlockSpec((1, t, MXU_DIM), lambda b, g, i: (b, 0, g)),
            pl.BlockSpec((1, HEADS_PER_SLAB, t), lambda b, g, i: (g, 0, b)),
        ],
        out_specs=pl.BlockSpec((1, tq, MXU_DIM), lambda b, g, i: (b, i, g)),
        out_shape=jax.ShapeDtypeStruct((nb, t, FOX_WIDTH), BF16),
        scratch_shapes=[pltpu.VMEM((tq, 1), F32), pltpu.VMEM((tq, 1), F32),
                        pltpu.VMEM((tq, MXU_DIM), F32)],
        compiler_params=_cparams(("parallel", "parallel", "arbitrary")),
        name="fox_prompt",
    )(q, k, v, c)


def _fox_sample_kernel(pt_ref, q_ref, kn_ref, vn_ref, lfn_ref, kc_hbm, vc_hbm, lc_hbm,
                       o_ref, kbuf, vbuf, lbuf, sem, m_sc, l_sc, acc_sc, *, ppc, nch, page):
    b = pl.program_id(0)
    nb = pl.num_programs(0)
    kc = ppc * page
    nrow = q_ref.shape[1]
    nt = nrow // FOX_HEADS

    def copies(bb, c, slot):
        out = []
        for p in range(ppc):
            pid = pt_ref[bb, c * ppc + p]
            out.append(pltpu.make_async_copy(kc_hbm.at[pid], kbuf.at[slot, p], sem.at[0, slot]))
            out.append(pltpu.make_async_copy(vc_hbm.at[pid], vbuf.at[slot, p], sem.at[1, slot]))
            out.append(pltpu.make_async_copy(lc_hbm.at[pid], lbuf.at[slot, p], sem.at[2, slot]))
        return out

    def start(bb, c, slot):
        for cp in copies(bb, c, slot):
            cp.start()

    def wait(bb, c, slot):
        for cp in copies(bb, c, slot):
            cp.wait()

    @pl.when(b == 0)
    def _():
        start(0, 0, 0)

    q = q_ref[0]
    tri = _tri_upper(page)
    m_sc[...] = jnp.full(m_sc.shape, NEG_INF, F32)
    l_sc[...] = jnp.zeros(l_sc.shape, F32)
    acc_sc[...] = jnp.zeros(acc_sc.shape, F32)

    def update(s, vv):
        m_old = m_sc[...]
        m_new = jnp.maximum(m_old, jnp.max(s, axis=-1, keepdims=True))
        alpha = jnp.exp(m_old - m_new)
        p = jnp.exp(s - m_new)
        l_sc[...] = alpha * l_sc[...] + jnp.sum(p, axis=-1, keepdims=True)
        acc_sc[...] = alpha * acc_sc[...] + _dot(p, vv)
        m_sc[...] = m_new

    carry = jnp.zeros((FOX_HEADS, 1), F32)
    for c in range(nch):
        slot = (b * nch + c) % 2
        if c + 1 < nch:
            start(b, c + 1, 1 - slot)
        else:
            @pl.when(b + 1 < nb)
            def _():
                start(b + 1, 0, 1 - slot)
        wait(b, c, slot)
        kk = kbuf[slot].reshape(kc, FOX_WIDTH)
        vv = vbuf[slot].reshape(kc, FOX_WIDTH)
        w = _cumsum_lanes(lbuf[slot].reshape(ppc * FOX_HEADS, page), tri)
        cs = []
        for p in range(ppc):
            wp = w[p * FOX_HEADS:(p + 1) * FOX_HEADS]
            cs.append(wp + carry)
            carry = carry + wp[:, page - 1:page]
        ck = jnp.concatenate(cs, axis=1)
        s = _dot_nt(q, kk) - jnp.concatenate([ck] * nt, axis=0)
        update(s, vv)

    lfn = lfn_ref[0]
    cn = carry + _cumsum_lanes(lfn, _tri_upper(LANES))
    sn = _dot_nt(q, kn_ref[0]) - jnp.concatenate([cn[:, 0:SUBLANES]] * nt, axis=0)
    r = lax.broadcasted_iota(jnp.int32, (nrow, SUBLANES), 0) // FOX_HEADS
    j = lax.broadcasted_iota(jnp.int32, (nrow, SUBLANES), 1)
    sn = jnp.where(j <= r, sn, NEG_INF)
    update(sn, vn_ref[0])

    res = acc_sc[...] / l_sc[...]
    lane_head = lax.broadcasted_iota(jnp.int32, (FOX_HEADS, FOX_WIDTH), 1) // FOX_HEAD_DIM
    sub_head = lax.broadcasted_iota(jnp.int32, (FOX_HEADS, FOX_WIDTH), 0)
    rows = []
    for t in range(nt):
        blk = res[t * FOX_HEADS:(t + 1) * FOX_HEADS]
        rows.append(jnp.sum(jnp.where(lane_head == sub_head, blk, 0.0), axis=0, keepdims=True))
    o_ref[0] = jnp.concatenate(rows, axis=0).astype(o_ref.dtype)


def _fox_sample(q, k_new, v_new, lft_new, cache_k, cache_v, cache_lf, page_table):
    nb, nt, _ = q.shape
    n_pool, page = cache_k.shape[0], cache_k.shape[1]
    n_pages = page_table.shape[1]
    ppc = min(FOX_PAGES_PER_CHUNK, n_pages)
    nch = n_pages // ppc
    nrow = nt * FOX_HEADS
    qf = q.astype(F32)
    head_of_lane = jnp.arange(FOX_WIDTH) // FOX_HEAD_DIM
    qrows = jnp.where(head_of_lane[None, None, None, :] == jnp.arange(FOX_HEADS)[None, None, :, None],
                      qf[:, :, None, :], 0.0).reshape(nb, nrow, FOX_WIDTH)
    pad = SUBLANES - nt
    kn = jnp.pad(k_new, ((0, 0), (0, pad), (0, 0)))
    vn = jnp.pad(v_new, ((0, 0), (0, pad), (0, 0)))
    lfn = jnp.pad(lft_new.reshape(FOX_HEADS, nb, nt).transpose(1, 0, 2),
                  ((0, 0), (0, 0), (0, LANES - nt)))
    kc2 = cache_k.reshape(n_pool, page, FOX_WIDTH)
    vc2 = cache_v.reshape(n_pool, page, FOX_WIDTH)
    lc2 = cache_lf.transpose(0, 2, 1)
    per_b = lambda shape: pl.BlockSpec((1,) + shape, lambda b, pt: (b, 0, 0))
    grid_spec = pltpu.PrefetchScalarGridSpec(
        num_scalar_prefetch=1,
        grid=(nb,),
        in_specs=[per_b((nrow, FOX_WIDTH)), per_b((SUBLANES, FOX_WIDTH)), per_b((SUBLANES, FOX_WIDTH)),
                  per_b((FOX_HEADS, LANES)),
                  pl.BlockSpec(memory_space=pl.ANY), pl.BlockSpec(memory_space=pl.ANY),
                  pl.BlockSpec(memory_space=pl.ANY)],
        out_specs=per_b((nt, FOX_WIDTH)),
        scratch_shapes=[
            pltpu.VMEM((2, ppc, page, FOX_WIDTH), F32),
            pltpu.VMEM((2, ppc, page, FOX_WIDTH), F32),
            pltpu.VMEM((2, ppc, FOX_HEADS, page), F32),
            pltpu.SemaphoreType.DMA((3, 2)),
            pltpu.VMEM((nrow, 1), F32), pltpu.VMEM((nrow, 1), F32),
            pltpu.VMEM((nrow, FOX_WIDTH), F32),
        ],
    )
    return pl.pallas_call(
        functools.partial(_fox_sample_kernel, ppc=ppc, nch=nch, page=page),
        grid_spec=grid_spec,
        out_shape=jax.ShapeDtypeStruct((nb, nt, FOX_WIDTH), BF16),
        compiler_params=_cparams(("arbitrary",)),
        name="fox_sample",
    )(page_table, qrows, kn, vn, lfn, kc2, vc2, lc2)


def _s5_kernel(u_ref, h0_ref, ar_ref, ai_ref, bm_ref, cm_ref, d_ref, wg_ref, bg_ref,
               o_ref, hf_ref, s_sc, h_sc, *, tt, nslab):
    ti = pl.program_id(1)
    ns = S5_NSTATE
    half = MXU_DIM
    hs = ns // 2

    @pl.when(ti == 0)
    def _():
        h_sc[...] = h0_ref[...]

    u = u_ref[...]
    ub = u.astype(BF16)
    for h in range(2):
        uh = ub[:, h * half:(h + 1) * half]
        s_sc[:, h * hs:(h + 1) * hs] = _dot(uh, bm_ref[2 * h])
        s_sc[:, ns + h * hs:ns + (h + 1) * hs] = _dot(uh, bm_ref[2 * h + 1])

    sw = ns // nslab
    for sl in range(nslab):
        lo = sl * sw
        ar = jnp.broadcast_to(ar_ref[:, lo:lo + sw], (SUBLANES, sw))
        ai = jnp.broadcast_to(ai_ref[:, lo:lo + sw], (SUBLANES, sw))

        def body(t, carry):
            hr, hi = carry
            r0 = pl.multiple_of(t * SUBLANES, SUBLANES)
            nr = ar * hr - ai * hi + s_sc[pl.ds(r0, SUBLANES), lo:lo + sw]
            ni = ar * hi + ai * hr + s_sc[pl.ds(r0, SUBLANES), ns + lo:ns + lo + sw]
            s_sc[pl.ds(r0, SUBLANES), lo:lo + sw] = nr
            s_sc[pl.ds(r0, SUBLANES), ns + lo:ns + lo + sw] = ni
            return nr, ni

        hr, hi = lax.fori_loop(0, tt, body, (h_sc[:, lo:lo + sw], h_sc[:, ns + lo:ns + lo + sw]))
        h_sc[:, lo:lo + sw] = hr
        h_sc[:, ns + lo:ns + lo + sw] = hi

    ys = []
    for h in range(2):
        yr = _dot(s_sc[:, h * hs:(h + 1) * hs].astype(BF16), cm_ref[2 * h])
        yi = _dot(s_sc[:, ns + h * hs:ns + (h + 1) * hs].astype(BF16), cm_ref[2 * h + 1])
        ys.append(yr + yi)
    y = jnp.concatenate(ys, axis=1) + d_ref[...] * u
    zg = 0.5 * y * (1.0 + jnp.tanh(math.sqrt(2.0 / math.pi) * (y + 0.044715 * (y * y * y))))
    gl = _dot(zg.astype(BF16), wg_ref[...]) + bg_ref[...]
    o_ref[...] = (zg * _sigmoid(gl)).astype(o_ref.dtype)

    @pl.when(ti == pl.num_programs(1) - 1)
    def _():
        hf_ref[...] = h_sc[...]


def _s5(u_tb, h0, ar, ai, bmat, cmat, d, wglu, bglu, nbg, t):
    tt = min(S5_TT, t)
    nti = t // tt
    rows = tt * SUBLANES
    full = lambda a: pl.BlockSpec(a.shape, lambda g, i: (0,) * a.ndim)
    tok = pl.BlockSpec((rows, S5_WIDTH), lambda g, i: (g * nti + i, 0))
    st = pl.BlockSpec((SUBLANES, 2 * S5_NSTATE), lambda g, i: (g, 0))
    return pl.pallas_call(
        functools.partial(_s5_kernel, tt=tt, nslab=4),
        grid=(nbg, nti),
        in_specs=[tok, st, full(ar), full(ai), full(bmat), full(cmat), full(d), full(wglu), full(bglu)],
        out_specs=(tok, st),
        out_shape=(jax.ShapeDtypeStruct(u_tb.shape, BF16),
                   jax.ShapeDtypeStruct(h0.shape, F32)),
        scratch_shapes=[pltpu.VMEM((rows, 2 * S5_NSTATE), F32),
                        pltpu.VMEM((SUBLANES, 2 * S5_NSTATE), F32)],
        compiler_params=_cparams(("parallel", "arbitrary")),
        name="s5",
    )(u_tb, h0, ar, ai, bmat, cmat, d, wglu, bglu)


def _memkv_kernel(m_ref, g_ref, w_ref, k32_ref, v32_ref, kb_ref, vb_ref):
    xn = _rms(m_ref[...], g_ref[...]).astype(BF16)
    k = _dot(xn, w_ref[:, 0:MEM_WIDTH])
    v = _dot(xn, w_ref[:, MEM_WIDTH:2 * MEM_WIDTH])
    k32_ref[...] = k
    v32_ref[...] = v
    kb_ref[...] = k.astype(BF16)
    vb_ref[...] = v.astype(BF16)


def _memkv(mem2d, g_mem, w_kv):
    n, d = mem2d.shape
    tm = min(PROJ_TM, n)
    g2 = g_mem.reshape(1, d)
    tok = lambda width: pl.BlockSpec((tm, width), lambda i: (i, 0))
    full = lambda a: pl.BlockSpec(a.shape, lambda i: (0,) * a.ndim)
    sh = lambda dt: jax.ShapeDtypeStruct((n, MEM_WIDTH), dt)
    return pl.pallas_call(
        _memkv_kernel,
        grid=(n // tm,),
        in_specs=[tok(d), full(g2), full(w_kv)],
        out_specs=(tok(MEM_WIDTH),) * 4,
        out_shape=(sh(F32), sh(F32), sh(BF16), sh(BF16)),
        compiler_params=_cparams(("parallel",)),
        name="mem_kv",
    )(mem2d, g2, w_kv)


def _memattn_kernel(q_ref, k_ref, v_ref, o_ref):
    q = q_ref[0]
    outs = []
    for h in range(MEM_HEADS):
        sl = slice(h * MEM_HEAD_DIM, (h + 1) * MEM_HEAD_DIM)
        kh = k_ref[0, :, sl].astype(BF16)
        vh = v_ref[0, :, sl].astype(BF16)
        s = _dot_nt(q[:, sl], kh)
        p = jnp.exp(s - jnp.max(s, axis=-1, keepdims=True))
        o = _dot(p.astype(BF16), vh)
        outs.append(o / jnp.sum(p, axis=-1, keepdims=True))
    o_ref[0] = jnp.concatenate(outs, axis=1).astype(o_ref.dtype)


def _memattn(qm, mk, mv):
    nb, t, _ = qm.shape
    n_mem = mk.shape[1]
    tq = min(MEMATTN_TQ, t)
    return pl.pallas_call(
        _memattn_kernel,
        grid=(nb, t // tq),
        in_specs=[pl.BlockSpec((1, tq, MEM_WIDTH), lambda b, i: (b, i, 0)),
                  pl.BlockSpec((1, n_mem, MEM_WIDTH), lambda b, i: (b, 0, 0)),
                  pl.BlockSpec((1, n_mem, MEM_WIDTH), lambda b, i: (b, 0, 0))],
        out_specs=pl.BlockSpec((1, tq, MEM_WIDTH), lambda b, i: (b, i, 0)),
        out_shape=jax.ShapeDtypeStruct(qm.shape, BF16),
        compiler_params=_cparams(("parallel", "parallel")),
        name="mem_attn",
    )(qm, mk, mv)


def _merge_kernel(x_ref, g_ref, wg_ref, of_ref, os_ref, om_ref, wf_ref, ws_ref, wm_ref, wo_ref, o_ref):
    x = x_ref[...]
    d = x.shape[1]
    xn = _rms(x, g_ref[...]).astype(BF16)
    merged = _sigmoid(_dot(xn, wg_ref[:, 0:d])) * _dot(of_ref[...], wf_ref[...])
    merged += _sigmoid(_dot(xn, wg_ref[:, d:2 * d])) * _dot(os_ref[...], ws_ref[...])
    merged += _sigmoid(_dot(xn, wg_ref[:, 2 * d:3 * d])) * _dot(om_ref[...], wm_ref[...])
    o_ref[...] = x + _dot(merged.astype(BF16), wo_ref[...])


def _merge(x2d, g_mix, w_gate, o_fox, o_s5, o_mem, w_f, w_s, w_m, w_out):
    n, d = x2d.shape
    tm = min(MERGE_TM, n)
    g2 = g_mix.reshape(1, d)
    tok = lambda width: pl.BlockSpec((tm, width), lambda i: (i, 0))
    full = lambda a: pl.BlockSpec(a.shape, lambda i: (0,) * a.ndim)
    return pl.pallas_call(
        _merge_kernel,
        grid=(n // tm,),
        in_specs=[tok(d), full(g2), full(w_gate), tok(FOX_WIDTH), tok(S5_WIDTH), tok(MEM_WIDTH),
                  full(w_f), full(w_s), full(w_m), full(w_out)],
        out_specs=tok(d),
        out_shape=jax.ShapeDtypeStruct((n, d), F32),
        compiler_params=_cparams(("parallel",)),
        name="merge",
    )(x2d, g2, w_gate, o_fox, o_s5, o_mem, w_f, w_s, w_m, w_out)


def _first_argmax(vals):
    m = vals[0]
    for v in vals[1:]:
        m = jnp.maximum(m, v)
    idx = jnp.full(m.shape, len(vals) - 1, jnp.int32)
    for i in range(len(vals) - 2, -1, -1):
        idx = jnp.where(vals[i] == m, i, idx)
    return idx, m


def _moe_kernel(x_ref, g_ref, wr_ref, br_ref, wg_ref, wu_ref, wd_ref, gf_ref, o_ref,
                h_sc, r_sc, acc_sc):
    j = pl.program_id(1)
    ng, ne = N_EXPERT_GROUPS, EXPERTS_PER_GROUP

    @pl.when(j == 0)
    def _():
        h = _rms(x_ref[...], g_ref[...]).astype(BF16)
        h_sc[...] = h
        lg = _dot(h, wr_ref[...]) + br_ref[...]
        col = lambda i: lg[:, i:i + 1]
        gl = [col(i) for i in range(ng)]
        gidx, gm = _first_argmax(gl)
        den = jnp.exp(gl[0] - gm)
        for v in gl[1:]:
            den = den + jnp.exp(v - gm)
        p_sel = 1.0 / den
        le = []
        for e in range(ne):
            v = col(ng + e)
            for g in range(1, ng):
                v = jnp.where(gidx == g, col(ng + g * ne + e), v)
            le.append(v)
        i1, v1 = _first_argmax(le)
        rest = [jnp.where(i1 == e, -jnp.inf, le[e]) for e in range(ne)]
        i2, v2 = _first_argmax(rest)
        t = jnp.exp(v2 - v1)
        w1 = 1.0 / (1.0 + t)
        w2 = t / (1.0 + t)
        lane = lax.broadcasted_iota(jnp.int32, lg.shape, 1)
        r = jnp.where(lane == i1, w1, 0.0) + jnp.where(lane == i2, w2, 0.0)
        r = jnp.where(lane == ne, p_sel, r)
        r = jnp.where(lane == ne + 1, gidx.astype(F32), r)
        r_sc[...] = r
        acc_sc[...] = jnp.zeros(acc_sc.shape, F32)

    h = h_sc[...]
    r = r_sc[...]
    scale = jnp.where(r[:, ne + 1:ne + 2] == j.astype(F32), r[:, ne:ne + 1], 0.0)
    acts = []
    for e in range(ne):
        sl = slice(e * D_EXPERT, (e + 1) * D_EXPERT)
        hg = _dot(h, wg_ref[:, sl])
        hu = _dot(h, wu_ref[:, sl])
        acts.append((hg * _sigmoid(hg) * hu * (scale * r[:, e:e + 1])).astype(BF16))
    acc_sc[...] += _dot(jnp.concatenate(acts, axis=1), wd_ref[...])

    @pl.when(j == ng - 1)
    def _():
        o_ref[...] = _rms(x_ref[...] + acc_sc[...], gf_ref[...])


def _moe(x2d, g_ffn, w_r, b_r, w_g, w_u, w_d, g_final):
    n, d = x2d.shape
    tm = min(MOE_TM, n)
    gw = EXPERTS_PER_GROUP * D_EXPERT
    g2 = g_ffn.reshape(1, d)
    gf2 = g_final.reshape(1, d)
    tok = pl.BlockSpec((tm, d), lambda i, j: (i, 0))
    full = lambda a: pl.BlockSpec(a.shape, lambda i, j: (0,) * a.ndim)
    return pl.pallas_call(
        _moe_kernel,
        grid=(n // tm, N_EXPERT_GROUPS),
        in_specs=[tok, full(g2), full(w_r), full(b_r),
                  pl.BlockSpec((d, gw), lambda i, j: (0, j)),
                  pl.BlockSpec((d, gw), lambda i, j: (0, j)),
                  pl.BlockSpec((gw, d), lambda i, j: (j, 0)),
                  full(gf2)],
        out_specs=tok,
        out_shape=jax.ShapeDtypeStruct((n, d), F32),
        scratch_shapes=[pltpu.VMEM((tm, d), BF16), pltpu.VMEM((tm, LANES), F32),
                        pltpu.VMEM((tm, d), F32)],
        compiler_params=_cparams(("parallel", "arbitrary")),
        name="moe",
    )(x2d, g2, w_r, b_r, w_g, w_u, w_d, gf2)


def _block_diag(blocks):
    nblk, r, c = blocks.shape
    eye = jnp.eye(nblk, dtype=blocks.dtype)
    return (eye[:, None, :, None] * blocks[:, :, None, :]).reshape(nblk * r, nblk * c)


def _prep_params(g_mix, w_in, b_forget, s5_a_re, s5_a_im, s5_log_dt, s5_b_re, s5_b_im, s5_c_re, s5_c_im,
                 s5_d, s5_w_glu, s5_b_glu, w_br_fox, w_br_s5, w_br_mem, w_out,
                 w_router_group, b_router_group, w_router_expert, b_router_expert,
                 w_exp_gate, w_exp_up, w_exp_down):
    d = w_in.shape[0]
    o3 = 3 * FOX_WIDTH
    o4 = o3 + FOX_HEADS
    o6 = o4 + S5_WIDTH + MEM_WIDTH
    p = {}
    p['w_a'] = jnp.concatenate([w_in[:, :o3], w_in[:, o4:o6]], axis=1).astype(BF16)
    wft = jnp.zeros((2 * SUBLANES, d), F32).at[:FOX_HEADS].set(w_in[:, o3:o4].T)
    p['wft'] = wft.astype(BF16)
    p['w_gate'] = w_in[:, o6:].astype(BF16)
    a_re = jnp.minimum(s5_a_re.astype(F32), A_RE_MAX)
    a_im = s5_a_im.astype(F32)
    dt = jnp.exp(s5_log_dt.astype(F32))[:, None]
    mag = jnp.exp(dt * a_re)
    ang = dt * a_im
    ab_re = mag * jnp.cos(ang)
    ab_im = mag * jnp.sin(ang)
    den = a_re * a_re + a_im * a_im
    n_re = ab_re - 1.0
    n_im = ab_im
    s_re = (n_re * a_re + n_im * a_im) / den
    s_im = (n_im * a_re - n_re * a_im) / den
    bb_re = s_re[..., None] * s5_b_re - s_im[..., None] * s5_b_im
    bb_im = s_re[..., None] * s5_b_im + s_im[..., None] * s5_b_re
    p['ar'] = ab_re.reshape(1, S5_NSTATE)
    p['ai'] = ab_im.reshape(1, S5_NSTATE)
    gh = S5_GROUPS // 2
    bm, cm = [], []
    for h in range(2):
        gs = slice(h * gh, (h + 1) * gh)
        bm.append(_block_diag(bb_re[gs].transpose(0, 2, 1)))
        bm.append(_block_diag(bb_im[gs].transpose(0, 2, 1)))
        cm.append(_block_diag(s5_c_re[gs].transpose(0, 2, 1)))
        cm.append(_block_diag(-s5_c_im[gs].transpose(0, 2, 1)))
    p['bmat'] = jnp.stack(bm).astype(BF16)
    p['cmat'] = jnp.stack(cm).astype(BF16)
    p['d'] = s5_d.reshape(1, S5_WIDTH).astype(F32)
    p['wglu'] = s5_w_glu.astype(BF16)
    p['bglu'] = s5_b_glu.reshape(1, S5_WIDTH).astype(F32)
    p['w_f'] = w_br_fox.astype(BF16)
    p['w_s'] = w_br_s5.astype(BF16)
    p['w_m'] = w_br_mem.astype(BF16)
    p['w_out'] = w_out.astype(BF16)
    nr = N_EXPERT_GROUPS * (1 + EXPERTS_PER_GROUP)
    w_r = jnp.concatenate([w_router_group, w_router_expert.reshape(d, -1)], axis=1)
    p['w_r'] = jnp.pad(w_r, ((0, 0), (0, LANES - nr))).astype(BF16)
    b_r = jnp.concatenate([b_router_group, b_router_expert.reshape(-1)])
    p['b_r'] = jnp.pad(b_r, (0, LANES - nr)).reshape(1, LANES).astype(F32)
    p['w_eg'] = w_exp_gate.astype(BF16)
    p['w_eu'] = w_exp_up.astype(BF16)
    p['w_ed'] = w_exp_down.astype(BF16)
    p['g_mix'] = g_mix
    p['b_forget'] = b_forget
    return p


def _to_time_major(a, nbg, t):
    w = a.shape[-1]
    return a.reshape(nbg, SUBLANES, t, w).transpose(0, 2, 1, 3).reshape(nbg * t * SUBLANES, w)


def _from_time_major(a, nbg, t):
    w = a.shape[-1]
    return a.reshape(nbg, t, SUBLANES, w).transpose(0, 2, 1, 3).reshape(nbg * SUBLANES, t, w)


def _layer(x, p, fox_fn, h0_re, h0_im, mem_k, mem_v, g_ffn, g_final):
    nb, t, d = x.shape
    n = nb * t
    x2d = x.reshape(n, d)
    q, k32, v32, kb, vb, u, qm, lft = _proj(x2d, p['g_mix'], p['w_a'], p['wft'], p['b_forget'])
    r3 = lambda a: a.reshape(nb, t, a.shape[-1])
    o_fox = fox_fn(r3(q), r3(kb), r3(vb), r3(k32), r3(v32), lft)
    nbg = nb // SUBLANES
    h0 = jnp.concatenate([h0_re.reshape(nb, S5_NSTATE), h0_im.reshape(nb, S5_NSTATE)], axis=1).astype(F32)
    o_s5_tb, hf = _s5(_to_time_major(r3(u), nbg, t), h0, p['ar'], p['ai'], p['bmat'], p['cmat'],
                      p['d'], p['wglu'], p['bglu'], nbg, t)
    o_s5 = _from_time_major(o_s5_tb, nbg, t)
    o_mem = _memattn(r3(qm), mem_k, mem_v)
    x1 = _merge(x2d, p['g_mix'], p['w_gate'], o_fox.reshape(n, FOX_WIDTH), o_s5.reshape(n, S5_WIDTH),
                o_mem.reshape(n, MEM_WIDTH), p['w_f'], p['w_s'], p['w_m'], p['w_out'])
    y = _moe(x1, g_ffn, p['w_r'], p['b_r'], p['w_eg'], p['w_eu'], p['w_ed'], g_final)
    k_out = k32.reshape(nb, t, FOX_HEADS, FOX_HEAD_DIM)
    v_out = v32.reshape(nb, t, FOX_HEADS, FOX_HEAD_DIM)
    lf_out = lft.reshape(FOX_HEADS, nb, t).transpose(1, 2, 0)
    h_re = hf[:, :S5_NSTATE].reshape(nb, S5_GROUPS, S5_STATE)
    h_im = hf[:, S5_NSTATE:].reshape(nb, S5_GROUPS, S5_STATE)
    return y.reshape(nb, t, d), k_out, v_out, lf_out, h_re, h_im


def kernel(x_prompt, x_sample, cache_fox_k, cache_fox_v, cache_fox_logf, cache_mem_k, cache_mem_v, state_s5_re, state_s5_im, page_table, mem_prompt, g_mix, w_in, b_forget, s5_a_re, s5_a_im, s5_log_dt, s5_b_re, s5_b_im, s5_c_re, s5_c_im, s5_d, s5_w_glu, s5_b_glu, g_mem, w_mem_kv, w_br_fox, w_br_s5, w_br_mem, w_out, g_ffn, w_router_group, b_router_group, w_router_expert, b_router_expert, w_exp_gate, w_exp_up, w_exp_down, g_final):
    depth = w_in.shape[0]
    assert depth == 1, "single-layer step"
    l = 0
    bp, sp, d = x_prompt.shape
    db, ds, _ = x_sample.shape
    n_mem = mem_prompt.shape[1]
    p = _prep_params(g_mix[l], w_in[l], b_forget[l], s5_a_re[l], s5_a_im[l], s5_log_dt[l], s5_b_re[l],
                     s5_b_im[l], s5_c_re[l], s5_c_im[l], s5_d[l], s5_w_glu[l], s5_b_glu[l],
                     w_br_fox[l], w_br_s5[l], w_br_mem[l], w_out[l],
                     w_router_group[l], b_router_group[l], w_router_expert[l], b_router_expert[l],
                     w_exp_gate[l], w_exp_up[l], w_exp_down[l])

    mk32, mv32, mkb, mvb = _memkv(mem_prompt.reshape(bp * n_mem, d), g_mem[l], w_mem_kv[l].astype(BF16))
    mem3 = lambda a: a.reshape(bp, n_mem, MEM_WIDTH)

    def fox_p(q, kb, vb, k32, v32, lft):
        c = _cumsum(lft, bp, sp)
        c = c.reshape(FOX_WIDTH // MXU_DIM, HEADS_PER_SLAB, bp * sp)
        return _fox_prompt(q, kb, vb, c, bp, sp)

    zero = jnp.zeros((bp, S5_GROUPS, S5_STATE), F32)
    yp, kp, vp, lfp, hr_p, hi_p = _layer(x_prompt, p, fox_p, zero, zero, mem3(mkb), mem3(mvb),
                                         g_ffn[l], g_final)

    def fox_s(q, kb, vb, k32, v32, lft):
        return _fox_sample(q, k32, v32, lft, cache_fox_k[l], cache_fox_v[l], cache_fox_logf[l], page_table)

    ys, ks, vs, lfs, hr_s, hi_s = _layer(x_sample, p, fox_s, state_s5_re[l], state_s5_im[l],
                                         cache_mem_k[l].reshape(db, n_mem, MEM_WIDTH),
                                         cache_mem_v[l].reshape(db, n_mem, MEM_WIDTH),
                                         g_ffn[l], g_final)
    e = lambda a: a[None]
    mk4 = mk32.reshape(bp, n_mem, MEM_HEADS, MEM_HEAD_DIM)
    mv4 = mv32.reshape(bp, n_mem, MEM_HEADS, MEM_HEAD_DIM)
    return (yp, ys, e(kp), e(vp), e(lfp), e(ks), e(vs), e(lfs), e(mk4), e(mv4),
            e(hr_p), e(hi_p), e(hr_s), e(hi_s))
```

```python
import functools
import math

import jax
import jax.numpy as jnp
from jax import lax
from jax.experimental import pallas as pl
from jax.experimental.pallas import tpu as pltpu

F32 = jnp.float32
BF16 = jnp.bfloat16

FOX_HEADS = 8
FOX_HEAD_DIM = 64
FOX_WIDTH = FOX_HEADS * FOX_HEAD_DIM
FOX_SCALE = FOX_HEAD_DIM ** -0.5
S5_GROUP = 16
S5_WIDTH = 512
S5_GROUPS = S5_WIDTH // S5_GROUP
S5_STATE = 64
S5_NSTATE = S5_GROUPS * S5_STATE
A_RE_MAX = -1e-4
MEM_HEADS = 4
MEM_HEAD_DIM = 128
MEM_WIDTH = MEM_HEADS * MEM_HEAD_DIM
MEM_SCALE = MEM_HEAD_DIM ** -0.5
N_BRANCH = 3
N_EXPERT_GROUPS = 4
EXPERTS_PER_GROUP = 4
D_EXPERT = 256
EPS = 1e-6
NEG_INF = -1e30
LOG2E = math.log2(math.e)

LANES = 128
SUBLANES = 8
MXU_DIM = 256
VMEM_LIMIT = 56 * 1024 * 1024

PROJ_TM = 512
CUMSUM_BLK = 512
FOX_TQ = 512
FOX_ROW_CHUNK = 32
FOX_PAGES_PER_CHUNK = 16
S5_TT = 64
MEMATTN_TQ = 1024
MERGE_TM = 512
MOE_TM = 512


def _cparams(sem):
    return pltpu.CompilerParams(dimension_semantics=sem, vmem_limit_bytes=VMEM_LIMIT)


def _rms(x, g):
    return x * lax.rsqrt(jnp.mean(x * x, axis=-1, keepdims=True) + EPS) * g


def _dot(a, b):
    return jnp.dot(a, b, preferred_element_type=F32)


def _dot_nt(a, b):
    return lax.dot_general(a, b, (((1,), (1,)), ((), ())), preferred_element_type=F32)


def _split3(x):
    hi = x.astype(BF16)
    r = x - hi.astype(F32)
    mid = r.astype(BF16)
    lo = (r - mid.astype(F32)).astype(BF16)
    return hi, mid, lo


def _sigmoid(x):
    return 1.0 / (1.0 + jnp.exp(-x))


def _proj_kernel(x_ref, g_ref, w_ref, wt_ref, wft_ref, bf_ref,
                 q_ref, kt32_ref, ktb_ref, vt32_ref, vb_ref, u_ref, qm_ref, lf_ref):
    xn = _rms(x_ref[...], g_ref[...]).astype(BF16)
    w = FOX_WIDTH
    q_ref[...] = (_dot(xn, w_ref[:, 0:w]) * (FOX_SCALE * LOG2E)).astype(BF16)
    vb_ref[...] = _dot(xn, w_ref[:, w:2 * w]).astype(BF16)
    u_ref[...] = _dot(xn, w_ref[:, 2 * w:2 * w + S5_WIDTH])
    o = 2 * w + S5_WIDTH
    qm_ref[...] = (_dot(xn, w_ref[:, o:o + MEM_WIDTH]) * MEM_SCALE).astype(BF16)
    kt = _dot_nt(wt_ref[0:w, :], xn)
    kt32_ref[0] = kt
    ktb_ref[0] = kt.astype(BF16)
    vt32_ref[0] = _dot_nt(wt_ref[w:2 * w, :], xn)
    zf = _dot_nt(wft_ref[...], xn)[0:FOX_HEADS, :] + bf_ref[...]
    lf_ref[...] = jnp.minimum(zf, 0.0) - jnp.log1p(jnp.exp(-jnp.abs(zf)))


def _proj(x2d, nb, t, g_mix, w_a, w_t, wft, b_forget):
    n, d = x2d.shape
    tm = min(PROJ_TM, t)
    nti = t // tm
    tok = lambda width: pl.BlockSpec((tm, width), lambda i: (i, 0))
    full = lambda a: pl.BlockSpec(a.shape, lambda i: (0,) * a.ndim)
    tr = pl.BlockSpec((1, FOX_WIDTH, tm), lambda i: (i // nti, 0, i % nti))
    trs = lambda dt: jax.ShapeDtypeStruct((nb, FOX_WIDTH, t), dt)
    out_shape = (
        jax.ShapeDtypeStruct((n, FOX_WIDTH), BF16),
        trs(F32),
        trs(BF16),
        trs(F32),
        jax.ShapeDtypeStruct((n, FOX_WIDTH), BF16),
        jax.ShapeDtypeStruct((n, S5_WIDTH), F32),
        jax.ShapeDtypeStruct((n, MEM_WIDTH), BF16),
        jax.ShapeDtypeStruct((FOX_HEADS, n), F32),
    )
    g2 = g_mix.reshape(1, d)
    bf2 = b_forget.reshape(FOX_HEADS, 1)
    return pl.pallas_call(
        _proj_kernel,
        grid=(n // tm,),
        in_specs=[tok(d), full(g2), full(w_a), full(w_t), full(wft), full(bf2)],
        out_specs=(tok(FOX_WIDTH), tr, tr, tr, tok(FOX_WIDTH),
                   tok(S5_WIDTH), tok(MEM_WIDTH), pl.BlockSpec((FOX_HEADS, tm), lambda i: (0, i))),
        out_shape=out_shape,
        compiler_params=_cparams(("parallel",)),
        name="proj",
    )(x2d, g2, w_a, w_t, wft, bf2)


def _tri_upper(n):
    r = lax.broadcasted_iota(jnp.int32, (n, n), 0)
    c = lax.broadcasted_iota(jnp.int32, (n, n), 1)
    return jnp.where(r <= c, 1.0, 0.0).astype(BF16)


def _cumsum_lanes(x, tri):
    hi, mid, lo = _split3(x)
    return _dot(hi, tri) + _dot(mid, tri) + _dot(lo, tri)


def _cumsum_kernel(lf_ref, c_ref, *, blk):
    t = lf_ref.shape[1]
    tri = _tri_upper(blk)

    def body(j, carry):
        off = pl.multiple_of(j * blk, blk)
        w = _cumsum_lanes(lf_ref[:, pl.ds(off, blk)], tri) + carry
        c_ref[:, pl.ds(off, blk)] = w
        return w[:, blk - 1:blk]

    lax.fori_loop(0, t // blk, body, jnp.zeros((FOX_HEADS, 1), F32))


def _cumsum(lft, nb, t):
    blk = min(CUMSUM_BLK, t)
    spec = pl.BlockSpec((FOX_HEADS, t), lambda b: (0, b))
    return pl.pallas_call(
        functools.partial(_cumsum_kernel, blk=blk),
        grid=(nb,),
        in_specs=[spec],
        out_specs=spec,
        out_shape=jax.ShapeDtypeStruct(lft.shape, F32),
        compiler_params=_cparams(("parallel",)),
        name="logf_cumsum",
    )(lft)


HEADS_PER_SLAB = MXU_DIM // FOX_HEAD_DIM


def _fox_kernel(q_ref, kt_ref, v_ref, c_ref, o_ref, qs_sc, p0_sc, p1_sc, m_sc, l_sc, acc_sc, *, tq):
    qi = pl.program_id(2)
    nh = HEADS_PER_SLAB
    q = q_ref[0]
    lane = lax.broadcasted_iota(jnp.int32, (1, MXU_DIM), 1)
    for h in range(nh):
        qs_sc[h * tq:(h + 1) * tq, :] = jnp.where((lane // FOX_HEAD_DIM) == h, q, jnp.zeros_like(q))
    q0 = pl.multiple_of(qi * tq, tq)
    cref = c_ref[0, :, pl.ds(q0, LANES)][:, 0:1]
    m_sc[...] = jnp.full(m_sc.shape, NEG_INF, F32)
    l_sc[...] = jnp.zeros(l_sc.shape, F32)
    acc_sc[...] = jnp.zeros(acc_sc.shape, F32)
    p1_sc[...] = jnp.zeros(p1_sc.shape, BF16)

    def values(j, p_in):
        off = pl.multiple_of(j * tq, tq)
        return _dot(p_in[...], v_ref[0, pl.ds(off, tq), :])

    def block(k, p_in, p_out, masked):
        pv = values(jnp.maximum(k - 1, 0), p_in)
        off = pl.multiple_of(k * tq, tq)
        s_all = _dot(qs_sc[...], kt_ref[0, :, pl.ds(off, tq)])
        bias = (cref - c_ref[0, :, pl.ds(off, tq)]) * LOG2E
        nr = FOX_ROW_CHUNK
        for h in range(nh):
            bias_h = bias[h:h + 1]
            for r0 in range(h * tq, (h + 1) * tq, nr):
                rows = slice(r0, r0 + nr)
                s = s_all[rows] + bias_h
                if masked:
                    row = lax.broadcasted_iota(jnp.int32, (nr, tq), 0) + (r0 - h * tq)
                    col = lax.broadcasted_iota(jnp.int32, (nr, tq), 1)
                    s = jnp.where(col <= row, s, NEG_INF)
                m_old = m_sc[rows, :]
                m_new = jnp.maximum(m_old, jnp.max(s, axis=-1, keepdims=True))
                alpha = jnp.exp2(m_old - m_new)
                p = jnp.exp2(s - jnp.tile(m_new, (1, tq // LANES)))
                l_sc[rows, :] = alpha * l_sc[rows, :] + jnp.sum(p, axis=-1, keepdims=True)
                m_sc[rows, :] = m_new
                p_out[rows, :] = p.astype(BF16)
                acc_sc[rows, :] = jnp.tile(alpha, (1, MXU_DIM // LANES)) * (acc_sc[rows, :] + pv[rows])

    def finish(k, p_in):
        acc = acc_sc[...] + values(k, p_in)
        inv = jnp.tile(1.0 / l_sc[...], (1, MXU_DIM // LANES))
        out = jnp.zeros((tq, MXU_DIM), F32)
        for h in range(nh):
            rows = slice(h * tq, (h + 1) * tq)
            out = jnp.where((lane // FOX_HEAD_DIM) == h, acc[rows] * inv[rows], out)
        o_ref[0] = out.astype(o_ref.dtype)

    def pair(kk, carry):
        block(2 * kk, p1_sc, p0_sc, False)
        block(2 * kk + 1, p0_sc, p1_sc, False)
        return carry

    lax.fori_loop(0, qi // 2, pair, 0)

    @pl.when(qi % 2 == 0)
    def _():
        block(qi, p1_sc, p0_sc, True)
        finish(qi, p0_sc)

    @pl.when(qi % 2 == 1)
    def _():
        block(qi - 1, p1_sc, p0_sc, False)
        block(qi, p0_sc, p1_sc, True)
        finish(qi, p1_sc)


def _fox_prompt(q, kt, v, c, nb, t):
    tq = min(FOX_TQ, t)
    nslab = FOX_WIDTH // MXU_DIM
    rows = HEADS_PER_SLAB * tq
    return pl.pallas_call(
        functools.partial(_fox_kernel, tq=tq),
        grid=(nb, nslab, t // tq),
        in_specs=[
            pl.BlockSpec((1, tq, MXU_DIM), lambda b, g, i: (b, i, g)),
            pl.BlockSpec((1, MXU_DIM, t), lambda b, g, i: (b, g, 0)),
            pl.BlockSpec((1, t, MXU_DIM), lambda b, g, i: (b, 0, g)),
            pl.BlockSpec((1, HEADS_PER_SLAB, t), lambda b, g, i: (g, 0, b)),
        ],
        out_specs=pl.BlockSpec((1, tq, MXU_DIM), lambda b, g, i: (b, i, g)),
        out_shape=jax.ShapeDtypeStruct((nb, t, FOX_WIDTH), BF16),
        scratch_shapes=[pltpu.VMEM((rows, MXU_DIM), BF16),
                        pltpu.VMEM((rows, tq), BF16),
                        pltpu.VMEM((rows, tq), BF16),
                        pltpu.VMEM((rows, LANES), F32), pltpu.VMEM((rows, LANES), F32),
                        pltpu.VMEM((rows, MXU_DIM), F32)],
        compiler_params=_cparams(("parallel", "parallel", "arbitrary")),
        name="fox_prompt",
    )(q, kt, v, c)


def _fox_sample_kernel(pt_ref, q_ref, kn_ref, vn_ref, lfn_ref, kc_hbm, vc_hbm, lc_hbm,
                       o_ref, kbuf, vbuf, lbuf, sem, m_sc, l_sc, acc_sc, *, ppc, nch, page):
    b = pl.program_id(0)
    nb = pl.num_programs(0)
    nrow = q_ref.shape[1]
    nt = nrow // FOX_HEADS

    def copies(bb, c, slot):
        out = []
        for p in range(ppc):
            pid = pt_ref[bb, c * ppc + p]
            out.append(pltpu.make_async_copy(kc_hbm.at[pid], kbuf.at[slot, p], sem.at[0, slot]))
            out.append(pltpu.make_async_copy(vc_hbm.at[pid], vbuf.at[slot, p], sem.at[1, slot]))
            out.append(pltpu.make_async_copy(lc_hbm.at[pid], lbuf.at[slot, p], sem.at[2, slot]))
        return out

    def start(bb, c, slot):
        for cp in copies(bb, c, slot):
            cp.start()

    def wait(bb, c, slot):
        for cp in copies(bb, c, slot):
            cp.wait()

    @pl.when(b == 0)
    def _():
        start(0, 0, 0)

    q = q_ref[0]
    tri = _tri_upper(page)
    m_sc[...] = jnp.full(m_sc.shape, NEG_INF, F32)
    l_sc[...] = jnp.zeros(l_sc.shape, F32)
    acc_sc[...] = jnp.zeros(acc_sc.shape, F32)

    def update(s, pv_fn):
        m_old = m_sc[...]
        m_new = jnp.maximum(m_old, jnp.max(s, axis=-1, keepdims=True))
        alpha = jnp.exp2(m_old - m_new)
        p = jnp.exp2(s - m_new)
        l_sc[...] = alpha * l_sc[...] + jnp.sum(p, axis=-1, keepdims=True)
        acc_sc[...] = alpha * acc_sc[...] + pv_fn(p)
        m_sc[...] = m_new

    carry = jnp.zeros((FOX_HEADS, 1), F32)
    for c in range(nch):
        slot = (b * nch + c) % 2
        if c + 1 < nch:
            start(b, c + 1, 1 - slot)
        else:
            @pl.when(b + 1 < nb)
            def _():
                start(b + 1, 0, 1 - slot)
        wait(b, c, slot)
        w = _cumsum_lanes(lbuf[slot].reshape(ppc * FOX_HEADS, page), tri)
        cs, ss = [], []
        for p in range(ppc):
            wp = w[p * FOX_HEADS:(p + 1) * FOX_HEADS]
            cs.append(wp + carry)
            carry = carry + wp[:, page - 1:page]
            ss.append(_dot(q, kbuf[slot, p]))
        ck = jnp.concatenate(cs, axis=1) * LOG2E
        s = jnp.concatenate(ss, axis=1) - jnp.concatenate([ck] * nt, axis=0)

        def pv_pages(pr, slot=slot):
            out = _dot_nt(pr[:, 0:page], vbuf[slot, 0])
            for p in range(1, ppc):
                out += _dot_nt(pr[:, p * page:(p + 1) * page], vbuf[slot, p])
            return out

        update(s, pv_pages)

    lfn = lfn_ref[0]
    cn = (carry + _cumsum_lanes(lfn, _tri_upper(LANES))) * LOG2E
    sn = _dot_nt(q, kn_ref[0]) - jnp.concatenate([cn[:, 0:SUBLANES]] * nt, axis=0)
    r = lax.broadcasted_iota(jnp.int32, (nrow, SUBLANES), 0) // FOX_HEADS
    j = lax.broadcasted_iota(jnp.int32, (nrow, SUBLANES), 1)
    sn = jnp.where(j <= r, sn, NEG_INF)
    update(sn, lambda pr: _dot(pr, vn_ref[0]))

    res = acc_sc[...] / l_sc[...]
    lane_head = lax.broadcasted_iota(jnp.int32, (FOX_HEADS, FOX_WIDTH), 1) // FOX_HEAD_DIM
    sub_head = lax.broadcasted_iota(jnp.int32, (FOX_HEADS, FOX_WIDTH), 0)
    rows = []
    for t in range(nt):
        blk = res[t * FOX_HEADS:(t + 1) * FOX_HEADS]
        rows.append(jnp.sum(jnp.where(lane_head == sub_head, blk, 0.0), axis=0, keepdims=True))
    o_ref[0] = jnp.concatenate(rows, axis=0).astype(o_ref.dtype)


def _fox_sample(q, k_new, v_new, lft_new, cache_k, cache_v, cache_lf, page_table):
    nb, nt, _ = q.shape
    n_pool, page = cache_k.shape[0], cache_k.shape[1]
    n_pages = page_table.shape[1]
    ppc = min(FOX_PAGES_PER_CHUNK, n_pages)
    nch = n_pages // ppc
    nrow = nt * FOX_HEADS
    qf = q.astype(F32)
    head_of_lane = jnp.arange(FOX_WIDTH) // FOX_HEAD_DIM
    qrows = jnp.where(head_of_lane[None, None, None, :] == jnp.arange(FOX_HEADS)[None, None, :, None],
                      qf[:, :, None, :], 0.0).reshape(nb, nrow, FOX_WIDTH)
    pad = SUBLANES - nt
    kn = jnp.pad(k_new, ((0, 0), (0, pad), (0, 0)))
    vn = jnp.pad(v_new, ((0, 0), (0, pad), (0, 0)))
    lfn = jnp.pad(lft_new.reshape(FOX_HEADS, nb, nt).transpose(1, 0, 2),
                  ((0, 0), (0, 0), (0, LANES - nt)))
    kc2 = cache_k.transpose(0, 2, 3, 1).reshape(n_pool, FOX_WIDTH, page)
    vc2 = cache_v.transpose(0, 2, 3, 1).reshape(n_pool, FOX_WIDTH, page)
    lc2 = cache_lf.transpose(0, 2, 1)
    per_b = lambda shape: pl.BlockSpec((1,) + shape, lambda b, pt: (b, 0, 0))
    grid_spec = pltpu.PrefetchScalarGridSpec(
        num_scalar_prefetch=1,
        grid=(nb,),
        in_specs=[per_b((nrow, FOX_WIDTH)), per_b((SUBLANES, FOX_WIDTH)), per_b((SUBLANES, FOX_WIDTH)),
                  per_b((FOX_HEADS, LANES)),
                  pl.BlockSpec(memory_space=pl.ANY), pl.BlockSpec(memory_space=pl.ANY),
                  pl.BlockSpec(memory_space=pl.ANY)],
        out_specs=per_b((nt, FOX_WIDTH)),
        scratch_shapes=[
            pltpu.VMEM((2, ppc, FOX_WIDTH, page), F32),
            pltpu.VMEM((2, ppc, FOX_WIDTH, page), F32),
            pltpu.VMEM((2, ppc, FOX_HEADS, page), F32),
            pltpu.SemaphoreType.DMA((3, 2)),
            pltpu.VMEM((nrow, 1), F32), pltpu.VMEM((nrow, 1), F32),
            pltpu.VMEM((nrow, FOX_WIDTH), F32),
        ],
    )
    return pl.pallas_call(
        functools.partial(_fox_sample_kernel, ppc=ppc, nch=nch, page=page),
        grid_spec=grid_spec,
        out_shape=jax.ShapeDtypeStruct((nb, nt, FOX_WIDTH), BF16),
        compiler_params=_cparams(("arbitrary",)),
        name="fox_sample",
    )(page_table, qrows, kn, vn, lfn, kc2, vc2, lc2)


def _s5_kernel(u_ref, h0_ref, ar_ref, ai_ref, bm_ref, cm_ref, d_ref, wg_ref, bg_ref,
               o_ref, hf_ref, s_sc, h_sc, *, tt, nslab):
    ti = pl.program_id(1)
    ns = S5_NSTATE
    half = MXU_DIM
    hs = ns // 2

    @pl.when(ti == 0)
    def _():
        h_sc[...] = h0_ref[...]

    u = u_ref[...]
    ub = u.astype(BF16)
    for h in range(2):
        uh = ub[:, h * half:(h + 1) * half]
        s_sc[:, h * hs:(h + 1) * hs] = _dot(uh, bm_ref[2 * h])
        s_sc[:, ns + h * hs:ns + (h + 1) * hs] = _dot(uh, bm_ref[2 * h + 1])

    sw = ns // nslab
    for sl in range(nslab):
        lo = sl * sw
        ar = jnp.broadcast_to(ar_ref[:, lo:lo + sw], (SUBLANES, sw))
        ai = jnp.broadcast_to(ai_ref[:, lo:lo + sw], (SUBLANES, sw))

        def body(t, carry):
            hr, hi = carry
            r0 = pl.multiple_of(t * SUBLANES, SUBLANES)
            nr = ar * hr - ai * hi + s_sc[pl.ds(r0, SUBLANES), lo:lo + sw]
            ni = ar * hi + ai * hr + s_sc[pl.ds(r0, SUBLANES), ns + lo:ns + lo + sw]
            s_sc[pl.ds(r0, SUBLANES), lo:lo + sw] = nr
            s_sc[pl.ds(r0, SUBLANES), ns + lo:ns + lo + sw] = ni
            return nr, ni

        hr, hi = lax.fori_loop(0, tt, body, (h_sc[:, lo:lo + sw], h_sc[:, ns + lo:ns + lo + sw]))
        h_sc[:, lo:lo + sw] = hr
        h_sc[:, ns + lo:ns + lo + sw] = hi

    ys = []
    for h in range(2):
        yr = _dot(s_sc[:, h * hs:(h + 1) * hs].astype(BF16), cm_ref[2 * h])
        yi = _dot(s_sc[:, ns + h * hs:ns + (h + 1) * hs].astype(BF16), cm_ref[2 * h + 1])
        ys.append(yr + yi)
    y = jnp.concatenate(ys, axis=1) + d_ref[...] * u
    zg = 0.5 * y * (1.0 + jnp.tanh(math.sqrt(2.0 / math.pi) * (y + 0.044715 * (y * y * y))))
    gl = _dot(zg.astype(BF16), wg_ref[...]) + bg_ref[...]
    o_ref[...] = (zg * _sigmoid(gl)).astype(o_ref.dtype)

    @pl.when(ti == pl.num_programs(1) - 1)
    def _():
        hf_ref[...] = h_sc[...]


def _s5(u_tb, h0, ar, ai, bmat, cmat, d, wglu, bglu, nbg, t):
    tt = min(S5_TT, t)
    nti = t // tt
    rows = tt * SUBLANES
    full = lambda a: pl.BlockSpec(a.shape, lambda g, i: (0,) * a.ndim)
    tok = pl.BlockSpec((rows, S5_WIDTH), lambda g, i: (g * nti + i, 0))
    st = pl.BlockSpec((SUBLANES, 2 * S5_NSTATE), lambda g, i: (g, 0))
    return pl.pallas_call(
        functools.partial(_s5_kernel, tt=tt, nslab=4),
        grid=(nbg, nti),
        in_specs=[tok, st, full(ar), full(ai), full(bmat), full(cmat), full(d), full(wglu), full(bglu)],
        out_specs=(tok, st),
        out_shape=(jax.ShapeDtypeStruct(u_tb.shape, BF16),
                   jax.ShapeDtypeStruct(h0.shape, F32)),
        scratch_shapes=[pltpu.VMEM((rows, 2 * S5_NSTATE), F32),
                        pltpu.VMEM((SUBLANES, 2 * S5_NSTATE), F32)],
        compiler_params=_cparams(("parallel", "arbitrary")),
        name="s5",
    )(u_tb, h0, ar, ai, bmat, cmat, d, wglu, bglu)


def _memkv_kernel(m_ref, g_ref, w_ref, k32_ref, v32_ref, kb_ref, vb_ref):
    xn = _rms(m_ref[...], g_ref[...]).astype(BF16)
    k = _dot(xn, w_ref[:, 0:MEM_WIDTH])
    v = _dot(xn, w_ref[:, MEM_WIDTH:2 * MEM_WIDTH])
    k32_ref[...] = k
    v32_ref[...] = v
    kb_ref[...] = k.astype(BF16)
    vb_ref[...] = v.astype(BF16)


def _memkv(mem2d, g_mem, w_kv):
    n, d = mem2d.shape
    tm = min(PROJ_TM, n)
    g2 = g_mem.reshape(1, d)
    tok = lambda width: pl.BlockSpec((tm, width), lambda i: (i, 0))
    full = lambda a: pl.BlockSpec(a.shape, lambda i: (0,) * a.ndim)
    sh = lambda dt: jax.ShapeDtypeStruct((n, MEM_WIDTH), dt)
    return pl.pallas_call(
        _memkv_kernel,
        grid=(n // tm,),
        in_specs=[tok(d), full(g2), full(w_kv)],
        out_specs=(tok(MEM_WIDTH),) * 4,
        out_shape=(sh(F32), sh(F32), sh(BF16), sh(BF16)),
        compiler_params=_cparams(("parallel",)),
        name="mem_kv",
    )(mem2d, g2, w_kv)


def _memattn_kernel(q_ref, k_ref, v_ref, o_ref):
    q = q_ref[0]
    outs = []
    for h in range(MEM_HEADS):
        sl = slice(h * MEM_HEAD_DIM, (h + 1) * MEM_HEAD_DIM)
        kh = k_ref[0, :, sl].astype(BF16)
        vh = v_ref[0, :, sl].astype(BF16)
        s = _dot_nt(q[:, sl], kh)
        p = jnp.exp(s - jnp.max(s, axis=-1, keepdims=True))
        o = _dot(p.astype(BF16), vh)
        outs.append(o / jnp.sum(p, axis=-1, keepdims=True))
    o_ref[0] = jnp.concatenate(outs, axis=1).astype(o_ref.dtype)


def _memattn(qm, mk, mv):
    nb, t, _ = qm.shape
    n_mem = mk.shape[1]
    tq = min(MEMATTN_TQ, t)
    return pl.pallas_call(
        _memattn_kernel,
        grid=(nb, t // tq),
        in_specs=[pl.BlockSpec((1, tq, MEM_WIDTH), lambda b, i: (b, i, 0)),
                  pl.BlockSpec((1, n_mem, MEM_WIDTH), lambda b, i: (b, 0, 0)),
                  pl.BlockSpec((1, n_mem, MEM_WIDTH), lambda b, i: (b, 0, 0))],
        out_specs=pl.BlockSpec((1, tq, MEM_WIDTH), lambda b, i: (b, i, 0)),
        out_shape=jax.ShapeDtypeStruct(qm.shape, BF16),
        compiler_params=_cparams(("parallel", "parallel")),
        name="mem_attn",
    )(qm, mk, mv)


def _merge_kernel(x_ref, g_ref, wg_ref, of_ref, os_ref, om_ref, wf_ref, ws_ref, wm_ref, wo_ref, o_ref):
    x = x_ref[...]
    d = x.shape[1]
    xn = _rms(x, g_ref[...]).astype(BF16)
    merged = _sigmoid(_dot(xn, wg_ref[:, 0:d])) * _dot(of_ref[...], wf_ref[...])
    merged += _sigmoid(_dot(xn, wg_ref[:, d:2 * d])) * _dot(os_ref[...], ws_ref[...])
    merged += _sigmoid(_dot(xn, wg_ref[:, 2 * d:3 * d])) * _dot(om_ref[...], wm_ref[...])
    o_ref[...] = x + _dot(merged.astype(BF16), wo_ref[...])


def _merge(x2d, g_mix, w_gate, o_fox, o_s5, o_mem, w_f, w_s, w_m, w_out):
    n, d = x2d.shape
    tm = min(MERGE_TM, n)
    g2 = g_mix.reshape(1, d)
    tok = lambda width: pl.BlockSpec((tm, width), lambda i: (i, 0))
    full = lambda a: pl.BlockSpec(a.shape, lambda i: (0,) * a.ndim)
    return pl.pallas_call(
        _merge_kernel,
        grid=(n // tm,),
        in_specs=[tok(d), full(g2), full(w_gate), tok(FOX_WIDTH), tok(S5_WIDTH), tok(MEM_WIDTH),
                  full(w_f), full(w_s), full(w_m), full(w_out)],
        out_specs=tok(d),
        out_shape=jax.ShapeDtypeStruct((n, d), F32),
        compiler_params=_cparams(("parallel",)),
        name="merge",
    )(x2d, g2, w_gate, o_fox, o_s5, o_mem, w_f, w_s, w_m, w_out)


def _first_argmax(vals):
    m = vals[0]
    for v in vals[1:]:
        m = jnp.maximum(m, v)
    idx = jnp.full(m.shape, len(vals) - 1, jnp.int32)
    for i in range(len(vals) - 2, -1, -1):
        idx = jnp.where(vals[i] == m, i, idx)
    return idx, m


def _moe_kernel(x_ref, g_ref, wr_ref, wrl_ref, br_ref, wg_ref, wu_ref, wd_ref, gf_ref, o_ref,
                h_sc, r_sc, acc_sc):
    j = pl.program_id(1)
    ng, ne = N_EXPERT_GROUPS, EXPERTS_PER_GROUP

    @pl.when(j == 0)
    def _():
        hf = _rms(x_ref[...], g_ref[...])
        h = hf.astype(BF16)
        h_sc[...] = h
        h_lo = (hf - h.astype(F32)).astype(BF16)
        lg = _dot(h, wr_ref[...]) + _dot(h_lo, wr_ref[...]) + _dot(h, wrl_ref[...]) + br_ref[...]
        col = lambda i: lg[:, i:i + 1]
        gl = [col(i) for i in range(ng)]
        gidx, gm = _first_argmax(gl)
        den = jnp.exp(gl[0] - gm)
        for v in gl[1:]:
            den = den + jnp.exp(v - gm)
        p_sel = 1.0 / den
        le = []
        for e in range(ne):
            v = col(ng + e)
            for g in range(1, ng):
                v = jnp.where(gidx == g, col(ng + g * ne + e), v)
            le.append(v)
        i1, v1 = _first_argmax(le)
        rest = [jnp.where(i1 == e, -jnp.inf, le[e]) for e in range(ne)]
        i2, v2 = _first_argmax(rest)
        t = jnp.exp(v2 - v1)
        w1 = 1.0 / (1.0 + t)
        w2 = t / (1.0 + t)
        lane = lax.broadcasted_iota(jnp.int32, lg.shape, 1)
        r = jnp.where(lane == i1, w1, 0.0) + jnp.where(lane == i2, w2, 0.0)
        r = jnp.where(lane == ne, p_sel, r)
        r = jnp.where(lane == ne + 1, gidx.astype(F32), r)
        r_sc[...] = r
        acc_sc[...] = jnp.zeros(acc_sc.shape, F32)

    h = h_sc[...]
    r = r_sc[...]
    scale = jnp.where(r[:, ne + 1:ne + 2] == j.astype(F32), r[:, ne:ne + 1], 0.0)
    acts = []
    for e in range(ne):
        sl = slice(e * D_EXPERT, (e + 1) * D_EXPERT)
        hg = _dot(h, wg_ref[:, sl])
        hu = _dot(h, wu_ref[:, sl])
        acts.append((hg * _sigmoid(hg) * hu * (scale * r[:, e:e + 1])).astype(BF16))
    acc_sc[...] += _dot(jnp.concatenate(acts, axis=1), wd_ref[...])

    @pl.when(j == ng - 1)
    def _():
        o_ref[...] = _rms(x_ref[...] + acc_sc[...], gf_ref[...])


def _moe(x2d, g_ffn, w_r, w_r_lo, b_r, w_g, w_u, w_d, g_final):
    n, d = x2d.shape
    tm = min(MOE_TM, n)
    gw = EXPERTS_PER_GROUP * D_EXPERT
    g2 = g_ffn.reshape(1, d)
    gf2 = g_final.reshape(1, d)
    tok = pl.BlockSpec((tm, d), lambda i, j: (i, 0))
    full = lambda a: pl.BlockSpec(a.shape, lambda i, j: (0,) * a.ndim)
    return pl.pallas_call(
        _moe_kernel,
        grid=(n // tm, N_EXPERT_GROUPS),
        in_specs=[tok, full(g2), full(w_r), full(w_r_lo), full(b_r),
                  pl.BlockSpec((d, gw), lambda i, j: (0, j)),
                  pl.BlockSpec((d, gw), lambda i, j: (0, j)),
                  pl.BlockSpec((gw, d), lambda i, j: (j, 0)),
                  full(gf2)],
        out_specs=tok,
        out_shape=jax.ShapeDtypeStruct((n, d), F32),
        scratch_shapes=[pltpu.VMEM((tm, d), BF16), pltpu.VMEM((tm, LANES), F32),
                        pltpu.VMEM((tm, d), F32)],
        compiler_params=_cparams(("parallel", "arbitrary")),
        name="moe",
    )(x2d, g2, w_r, w_r_lo, b_r, w_g, w_u, w_d, gf2)


def _block_diag(blocks):
    nblk, r, c = blocks.shape
    eye = jnp.eye(nblk, dtype=blocks.dtype)
    return (eye[:, None, :, None] * blocks[:, :, None, :]).reshape(nblk * r, nblk * c)


def _prep_params(g_mix, w_in, b_forget, s5_a_re, s5_a_im, s5_log_dt, s5_b_re, s5_b_im, s5_c_re, s5_c_im,
                 s5_d, s5_w_glu, s5_b_glu, w_br_fox, w_br_s5, w_br_mem, w_out,
                 w_router_group, b_router_group, w_router_expert, b_router_expert,
                 w_exp_gate, w_exp_up, w_exp_down):
    d = w_in.shape[0]
    o3 = 3 * FOX_WIDTH
    o4 = o3 + FOX_HEADS
    o6 = o4 + S5_WIDTH + MEM_WIDTH
    p = {}
    o1, o2 = FOX_WIDTH, 2 * FOX_WIDTH
    p['w_a'] = jnp.concatenate([w_in[:, :o1], w_in[:, o2:o3], w_in[:, o4:o6]], axis=1).astype(BF16)
    p['w_t'] = w_in[:, o1:o3].T.astype(BF16)
    wft = jnp.zeros((2 * SUBLANES, d), F32).at[:FOX_HEADS].set(w_in[:, o3:o4].T)
    p['wft'] = wft.astype(BF16)
    p['w_gate'] = w_in[:, o6:].astype(BF16)
    a_re = jnp.minimum(s5_a_re.astype(F32), A_RE_MAX)
    a_im = s5_a_im.astype(F32)
    dt = jnp.exp(s5_log_dt.astype(F32))[:, None]
    mag = jnp.exp(dt * a_re)
    ang = dt * a_im
    ab_re = mag * jnp.cos(ang)
    ab_im = mag * jnp.sin(ang)
    den = a_re * a_re + a_im * a_im
    n_re = ab_re - 1.0
    n_im = ab_im
    s_re = (n_re * a_re + n_im * a_im) / den
    s_im = (n_im * a_re - n_re * a_im) / den
    bb_re = s_re[..., None] * s5_b_re - s_im[..., None] * s5_b_im
    bb_im = s_re[..., None] * s5_b_im + s_im[..., None] * s5_b_re
    p['ar'] = ab_re.reshape(1, S5_NSTATE)
    p['ai'] = ab_im.reshape(1, S5_NSTATE)
    gh = S5_GROUPS // 2
    bm, cm = [], []
    for h in range(2):
        gs = slice(h * gh, (h + 1) * gh)
        bm.append(_block_diag(bb_re[gs].transpose(0, 2, 1)))
        bm.append(_block_diag(bb_im[gs].transpose(0, 2, 1)))
        cm.append(_block_diag(s5_c_re[gs].transpose(0, 2, 1)))
        cm.append(_block_diag(-s5_c_im[gs].transpose(0, 2, 1)))
    p['bmat'] = jnp.stack(bm).astype(BF16)
    p['cmat'] = jnp.stack(cm).astype(BF16)
    p['d'] = s5_d.reshape(1, S5_WIDTH).astype(F32)
    p['wglu'] = s5_w_glu.astype(BF16)
    p['bglu'] = s5_b_glu.reshape(1, S5_WIDTH).astype(F32)
    p['w_f'] = w_br_fox.astype(BF16)
    p['w_s'] = w_br_s5.astype(BF16)
    p['w_m'] = w_br_mem.astype(BF16)
    p['w_out'] = w_out.astype(BF16)
    nr = N_EXPERT_GROUPS * (1 + EXPERTS_PER_GROUP)
    w_r = jnp.concatenate([w_router_group, w_router_expert.reshape(d, -1)], axis=1)
    w_r = jnp.pad(w_r, ((0, 0), (0, LANES - nr))).astype(F32)
    p['w_r'] = w_r.astype(BF16)
    p['w_r_lo'] = (w_r - p['w_r'].astype(F32)).astype(BF16)
    b_r = jnp.concatenate([b_router_group, b_router_expert.reshape(-1)])
    p['b_r'] = jnp.pad(b_r, (0, LANES - nr)).reshape(1, LANES).astype(F32)
    p['w_eg'] = w_exp_gate.astype(BF16)
    p['w_eu'] = w_exp_up.astype(BF16)
    p['w_ed'] = w_exp_down.astype(BF16)
    p['g_mix'] = g_mix
    p['b_forget'] = b_forget
    return p


def _to_time_major(a, nbg, t):
    w = a.shape[-1]
    return a.reshape(nbg, SUBLANES, t, w).transpose(0, 2, 1, 3).reshape(nbg * t * SUBLANES, w)


def _from_time_major(a, nbg, t):
    w = a.shape[-1]
    return a.reshape(nbg, t, SUBLANES, w).transpose(0, 2, 1, 3).reshape(nbg * SUBLANES, t, w)


def _layer(x, p, fox_fn, h0_re, h0_im, mem_k, mem_v, g_ffn, g_final):
    nb, t, d = x.shape
    n = nb * t
    x2d = x.reshape(n, d)
    pnb, pt = (nb, t) if t % LANES == 0 else (1, n)
    q, kt32, ktb, vt32, vb, u, qm, lft = _proj(x2d, pnb, pt, p['g_mix'], p['w_a'], p['w_t'], p['wft'],
                                               p['b_forget'])
    r3 = lambda a: a.reshape(nb, t, a.shape[-1])
    heads_last = lambda a: (a.reshape(pnb, FOX_HEADS, FOX_HEAD_DIM, pt).transpose(0, 3, 1, 2)
                            .reshape(nb, t, FOX_HEADS, FOX_HEAD_DIM))
    k_out = heads_last(kt32)
    v_out = heads_last(vt32)
    o_fox = fox_fn(r3(q), ktb, r3(vb), k_out.reshape(nb, t, FOX_WIDTH), v_out.reshape(nb, t, FOX_WIDTH), lft)
    nbg = nb // SUBLANES
    h0 = jnp.concatenate([h0_re.reshape(nb, S5_NSTATE), h0_im.reshape(nb, S5_NSTATE)], axis=1).astype(F32)
    o_s5_tb, hf = _s5(_to_time_major(r3(u), nbg, t), h0, p['ar'], p['ai'], p['bmat'], p['cmat'],
                      p['d'], p['wglu'], p['bglu'], nbg, t)
    o_s5 = _from_time_major(o_s5_tb, nbg, t)
    o_mem = _memattn(r3(qm), mem_k, mem_v)
    x1 = _merge(x2d, p['g_mix'], p['w_gate'], o_fox.reshape(n, FOX_WIDTH), o_s5.reshape(n, S5_WIDTH),
                o_mem.reshape(n, MEM_WIDTH), p['w_f'], p['w_s'], p['w_m'], p['w_out'])
    y = _moe(x1, g_ffn, p['w_r'], p['w_r_lo'], p['b_r'], p['w_eg'], p['w_eu'], p['w_ed'], g_final)
    lf_out = lft.reshape(FOX_HEADS, nb, t).transpose(1, 2, 0)
    h_re = hf[:, :S5_NSTATE].reshape(nb, S5_GROUPS, S5_STATE)
    h_im = hf[:, S5_NSTATE:].reshape(nb, S5_GROUPS, S5_STATE)
    return y.reshape(nb, t, d), k_out, v_out, lf_out, h_re, h_im


def kernel(x_prompt, x_sample, cache_fox_k, cache_fox_v, cache_fox_logf, cache_mem_k, cache_mem_v, state_s5_re, state_s5_im, page_table, mem_prompt, g_mix, w_in, b_forget, s5_a_re, s5_a_im, s5_log_dt, s5_b_re, s5_b_im, s5_c_re, s5_c_im, s5_d, s5_w_glu, s5_b_glu, g_mem, w_mem_kv, w_br_fox, w_br_s5, w_br_mem, w_out, g_ffn, w_router_group, b_router_group, w_router_expert, b_router_expert, w_exp_gate, w_exp_up, w_exp_down, g_final):
    depth = w_in.shape[0]
    assert depth == 1, "single-layer step"
    l = 0
    bp, sp, d = x_prompt.shape
    db, ds, _ = x_sample.shape
    n_mem = mem_prompt.shape[1]
    p = _prep_params(g_mix[l], w_in[l], b_forget[l], s5_a_re[l], s5_a_im[l], s5_log_dt[l], s5_b_re[l],
                     s5_b_im[l], s5_c_re[l], s5_c_im[l], s5_d[l], s5_w_glu[l], s5_b_glu[l],
                     w_br_fox[l], w_br_s5[l], w_br_mem[l], w_out[l],
                     w_router_group[l], b_router_group[l], w_router_expert[l], b_router_expert[l],
                     w_exp_gate[l], w_exp_up[l], w_exp_down[l])

    mk32, mv32, mkb, mvb = _memkv(mem_prompt.reshape(bp * n_mem, d), g_mem[l], w_mem_kv[l].astype(BF16))
    mem3 = lambda a: a.reshape(bp, n_mem, MEM_WIDTH)

    def fox_p(q, ktb, vb, k32, v32, lft):
        c = _cumsum(lft, bp, sp)
        c = c.reshape(FOX_WIDTH // MXU_DIM, HEADS_PER_SLAB, bp * sp)
        return _fox_prompt(q, ktb, vb, c, bp, sp)

    zero = jnp.zeros((bp, S5_GROUPS, S5_STATE), F32)
    yp, kp, vp, lfp, hr_p, hi_p = _layer(x_prompt, p, fox_p, zero, zero, mem3(mkb), mem3(mvb),
                                         g_ffn[l], g_final)

    def fox_s(q, ktb, vb, k32, v32, lft):
        return _fox_sample(q, k32, v32, lft, cache_fox_k[l], cache_fox_v[l], cache_fox_logf[l], page_table)

    ys, ks, vs, lfs, hr_s, hi_s = _layer(x_sample, p, fox_s, state_s5_re[l], state_s5_im[l],
                                         cache_mem_k[l].reshape(db, n_mem, MEM_WIDTH),
                                         cache_mem_v[l].reshape(db, n_mem, MEM_WIDTH),
                                         g_ffn[l], g_final)
    e = lambda a: a[None]
    mk4 = mk32.reshape(bp, n_mem, MEM_HEADS, MEM_HEAD_DIM)
    mv4 = mv32.reshape(bp, n_mem, MEM_HEADS, MEM_HEAD_DIM)
    return (yp, ys, e(kp), e(vp), e(lfp), e(ks), e(vs), e(lfs), e(mk4), e(mv4),
            e(hr_p), e(hi_p), e(hr_s), e(hi_s))
```

```python
import functools
import math

import jax
import jax.numpy as jnp
from jax import lax
from jax.experimental import pallas as pl
from jax.experimental.pallas import tpu as pltpu

F32 = jnp.float32
BF16 = jnp.bfloat16

FOX_HEADS = 8
FOX_HEAD_DIM = 64
FOX_WIDTH = FOX_HEADS * FOX_HEAD_DIM
FOX_SCALE = FOX_HEAD_DIM ** -0.5
S5_GROUP = 16
S5_WIDTH = 512
S5_GROUPS = S5_WIDTH // S5_GROUP
S5_STATE = 64
S5_NSTATE = S5_GROUPS * S5_STATE
A_RE_MAX = -1e-4
MEM_HEADS = 4
MEM_HEAD_DIM = 128
MEM_WIDTH = MEM_HEADS * MEM_HEAD_DIM
MEM_SCALE = MEM_HEAD_DIM ** -0.5
N_BRANCH = 3
N_EXPERT_GROUPS = 4
EXPERTS_PER_GROUP = 4
D_EXPERT = 256
EPS = 1e-6
NEG_INF = -1e30
LOG2E = math.log2(math.e)

LANES = 128
SUBLANES = 8
MXU_DIM = 256
VMEM_LIMIT = 56 * 1024 * 1024

PROJ_TM = 512
CUMSUM_BLK = 512
FOX_TQ = 512
FOX_ROW_CHUNK = 16
FOX_PAGES_PER_CHUNK = 16
S5_TT = 64
MEMATTN_TQ = 1024
MERGE_TM = 512
MOE_TM = 1024


def _cparams(sem):
    return pltpu.CompilerParams(dimension_semantics=sem, vmem_limit_bytes=VMEM_LIMIT)


def _rms(x, g):
    return x * lax.rsqrt(jnp.mean(x * x, axis=-1, keepdims=True) + EPS) * g


def _dot(a, b):
    return jnp.dot(a, b, preferred_element_type=F32)


def _dot_nt(a, b):
    return lax.dot_general(a, b, (((1,), (1,)), ((), ())), preferred_element_type=F32)


def _split3(x):
    hi = x.astype(BF16)
    r = x - hi.astype(F32)
    mid = r.astype(BF16)
    lo = (r - mid.astype(F32)).astype(BF16)
    return hi, mid, lo


def _sigmoid(x):
    return 1.0 / (1.0 + jnp.exp(-x))


def _proj_kernel(x_ref, g_ref, w_ref, wt_ref, wft_ref, bf_ref,
                 q_ref, kt32_ref, ktb_ref, vt32_ref, vb_ref, u_ref, qm_ref, lf_ref):
    xn = _rms(x_ref[...], g_ref[...]).astype(BF16)
    w = FOX_WIDTH
    q_ref[...] = (_dot(xn, w_ref[:, 0:w]) * (FOX_SCALE * LOG2E)).astype(BF16)
    vb_ref[...] = _dot(xn, w_ref[:, w:2 * w]).astype(BF16)
    u_ref[...] = _dot(xn, w_ref[:, 2 * w:2 * w + S5_WIDTH])
    o = 2 * w + S5_WIDTH
    qm_ref[...] = (_dot(xn, w_ref[:, o:o + MEM_WIDTH]) * MEM_SCALE).astype(BF16)
    kt = _dot_nt(wt_ref[0:w, :], xn)
    kt32_ref[0] = kt
    ktb_ref[0] = kt.astype(BF16)
    vt32_ref[0] = _dot_nt(wt_ref[w:2 * w, :], xn)
    zf = _dot_nt(wft_ref[...], xn)[0:FOX_HEADS, :] + bf_ref[...]
    lf_ref[...] = jnp.minimum(zf, 0.0) - jnp.log1p(jnp.exp(-jnp.abs(zf)))


def _proj(x2d, nb, t, g_mix, w_a, w_t, wft, b_forget):
    n, d = x2d.shape
    tm = min(PROJ_TM, t)
    nti = t // tm
    tok = lambda width: pl.BlockSpec((tm, width), lambda i: (i, 0))
    full = lambda a: pl.BlockSpec(a.shape, lambda i: (0,) * a.ndim)
    tr = pl.BlockSpec((1, FOX_WIDTH, tm), lambda i: (i // nti, 0, i % nti))
    trs = lambda dt: jax.ShapeDtypeStruct((nb, FOX_WIDTH, t), dt)
    out_shape = (
        jax.ShapeDtypeStruct((n, FOX_WIDTH), BF16),
        trs(F32),
        trs(BF16),
        trs(F32),
        jax.ShapeDtypeStruct((n, FOX_WIDTH), BF16),
        jax.ShapeDtypeStruct((n, S5_WIDTH), F32),
        jax.ShapeDtypeStruct((n, MEM_WIDTH), BF16),
        jax.ShapeDtypeStruct((FOX_HEADS, n), F32),
    )
    g2 = g_mix.reshape(1, d)
    bf2 = b_forget.reshape(FOX_HEADS, 1)
    return pl.pallas_call(
        _proj_kernel,
        grid=(n // tm,),
        in_specs=[tok(d), full(g2), full(w_a), full(w_t), full(wft), full(bf2)],
        out_specs=(tok(FOX_WIDTH), tr, tr, tr, tok(FOX_WIDTH),
                   tok(S5_WIDTH), tok(MEM_WIDTH), pl.BlockSpec((FOX_HEADS, tm), lambda i: (0, i))),
        out_shape=out_shape,
        compiler_params=_cparams(("parallel",)),
        name="proj",
    )(x2d, g2, w_a, w_t, wft, bf2)


def _tri_upper(n):
    r = lax.broadcasted_iota(jnp.int32, (n, n), 0)
    c = lax.broadcasted_iota(jnp.int32, (n, n), 1)
    return jnp.where(r <= c, 1.0, 0.0).astype(BF16)


def _cumsum_lanes(x, tri):
    hi, mid, lo = _split3(x)
    return _dot(hi, tri) + _dot(mid, tri) + _dot(lo, tri)


def _cumsum_kernel(lf_ref, c_ref, *, blk):
    t = lf_ref.shape[1]
    tri = _tri_upper(blk)

    def body(j, carry):
        off = pl.multiple_of(j * blk, blk)
        w = _cumsum_lanes(lf_ref[:, pl.ds(off, blk)], tri) + carry
        c_ref[:, pl.ds(off, blk)] = w
        return w[:, blk - 1:blk]

    lax.fori_loop(0, t // blk, body, jnp.zeros((FOX_HEADS, 1), F32))


def _cumsum(lft, nb, t):
    blk = min(CUMSUM_BLK, t)
    spec = pl.BlockSpec((FOX_HEADS, t), lambda b: (0, b))
    return pl.pallas_call(
        functools.partial(_cumsum_kernel, blk=blk),
        grid=(nb,),
        in_specs=[spec],
        out_specs=spec,
        out_shape=jax.ShapeDtypeStruct(lft.shape, F32),
        compiler_params=_cparams(("parallel",)),
        name="logf_cumsum",
    )(lft)


HEADS_PER_SLAB = MXU_DIM // FOX_HEAD_DIM


def _fox_kernel(q_ref, kt_ref, v_ref, c_ref, o_ref, qs_sc, p0_sc, p1_sc, m_sc, acc_sc, *, tq):
    qi = pl.program_id(2)
    nh = HEADS_PER_SLAB
    q = q_ref[0]
    lane = lax.broadcasted_iota(jnp.int32, (1, MXU_DIM), 1)
    for h in range(nh):
        qs_sc[h * tq:(h + 1) * tq, :] = jnp.where((lane // FOX_HEAD_DIM) == h, q, jnp.zeros_like(q))
    q0 = pl.multiple_of(qi * tq, tq)
    cref = c_ref[0, :, pl.ds(q0, LANES)][:, 0:1]
    m_sc[...] = jnp.full(m_sc.shape, NEG_INF, F32)
    acc_sc[...] = jnp.zeros(acc_sc.shape, F32)
    p1_sc[...] = jnp.zeros(p1_sc.shape, BF16)

    wide = 2 * tq
    lane_head = lax.broadcasted_iota(jnp.int32, (1, LANES), 1) // FOX_HEAD_DIM

    def values(koff, width, p_in):
        v = v_ref[0, pl.ds(koff, width), :]
        out = []
        for h in range(nh):
            slab = v[:, (h // 2) * LANES:(h // 2 + 1) * LANES]
            vh = jnp.where(lane_head == h % 2, slab, jnp.ones_like(slab))
            out.append(_dot(p_in[h * tq:(h + 1) * tq, 0:width], vh))
        return out

    def block(koff, width, p_in, p_out, shift):
        koff = pl.multiple_of(koff, tq)
        s_all = _dot(qs_sc[...], kt_ref[0, :, pl.ds(koff, width)])
        pv = values(pl.multiple_of(jnp.maximum(koff - wide, 0), tq), wide, p_in)
        bias = (cref - c_ref[0, :, pl.ds(koff, width)]) * LOG2E
        nr = FOX_ROW_CHUNK
        for h in range(nh):
            bias_h = bias[h:h + 1]
            for r0 in range(h * tq, (h + 1) * tq, nr):
                rows = slice(r0, r0 + nr)
                s = s_all[rows] + bias_h
                if shift is not None:
                    row = lax.broadcasted_iota(jnp.int32, (nr, width), 0) + (r0 - h * tq + shift)
                    col = lax.broadcasted_iota(jnp.int32, (nr, width), 1)
                    s = jnp.where(col <= row, s, NEG_INF)
                m_old = m_sc[rows, :]
                m_new = jnp.maximum(m_old, jnp.max(s, axis=-1, keepdims=True))
                alpha = jnp.exp2(m_old - m_new)
                p_out[rows, 0:width] = jnp.exp2((s - jnp.tile(m_new, (1, width // LANES))).astype(BF16))
                m_sc[rows, :] = m_new
                acc_sc[rows, :] = alpha * (acc_sc[rows, :] + pv[h][r0 - h * tq:r0 - h * tq + nr])

    def finish(koff, width, p_in):
        pv = values(pl.multiple_of(koff, tq), width, p_in)
        acc = acc_sc[...] + jnp.concatenate(pv, axis=0)
        outs = []
        for g2 in range(nh // 2):
            even = acc[(2 * g2) * tq:(2 * g2 + 1) * tq]
            odd = acc[(2 * g2 + 1) * tq:(2 * g2 + 2) * tq]
            both = jnp.where(lane_head == 0, even, odd)
            sums = jnp.where(lane_head == 0, pltpu.roll(even, FOX_HEAD_DIM, 1),
                             pltpu.roll(odd, FOX_HEAD_DIM, 1))
            outs.append(both / sums)
        o_ref[0] = jnp.concatenate(outs, axis=1).astype(o_ref.dtype)

    nfull = qi // 2

    def pair(kk, carry):
        block(2 * kk * wide, wide, p1_sc, p0_sc, None)
        block((2 * kk + 1) * wide, wide, p0_sc, p1_sc, None)
        return carry

    lax.fori_loop(0, nfull // 2, pair, 0)

    for case in range(4):
        odd_full, odd_q = case // 2, case % 2
        twidth, tshift = (wide, tq) if odd_q else (tq, 0)

        @pl.when(qi % 4 == case)
        def _(odd_full=odd_full, twidth=twidth, tshift=tshift):
            toff = nfull * wide
            if odd_full:
                block(toff - wide, wide, p1_sc, p0_sc, None)
                block(toff, twidth, p0_sc, p1_sc, tshift)
                finish(toff, twidth, p1_sc)
            else:
                block(toff, twidth, p1_sc, p0_sc, tshift)
                finish(toff, twidth, p0_sc)


def _fox_prompt(q, kt, v, c, nb, t):
    tq = min(FOX_TQ, t // 2)
    assert t % (2 * tq) == 0 and tq % LANES == 0
    nslab = FOX_WIDTH // MXU_DIM
    rows = HEADS_PER_SLAB * tq
    return pl.pallas_call(
        functools.partial(_fox_kernel, tq=tq),
        grid=(nb, nslab, t // tq),
        in_specs=[
            pl.BlockSpec((1, tq, MXU_DIM), lambda b, g, i: (b, i, g)),
            pl.BlockSpec((1, MXU_DIM, t), lambda b, g, i: (b, g, 0)),
            pl.BlockSpec((1, t, MXU_DIM), lambda b, g, i: (b, 0, g)),
            pl.BlockSpec((1, HEADS_PER_SLAB, t), lambda b, g, i: (g, 0, b)),
        ],
        out_specs=pl.BlockSpec((1, tq, MXU_DIM), lambda b, g, i: (b, i, g)),
        out_shape=jax.ShapeDtypeStruct((nb, t, FOX_WIDTH), BF16),
        scratch_shapes=[pltpu.VMEM((rows, MXU_DIM), BF16),
                        pltpu.VMEM((rows, 2 * tq), BF16),
                        pltpu.VMEM((rows, 2 * tq), BF16),
                        pltpu.VMEM((rows, LANES), F32),
                        pltpu.VMEM((rows, LANES), F32)],
        compiler_params=_cparams(("parallel", "parallel", "arbitrary")),
        name="fox_prompt",
    )(q, kt, v, c)


def _fox_sample_kernel(pt_ref, q_ref, kn_ref, vn_ref, lfn_ref, kc_hbm, vc_hbm, lc_hbm,
                       o_ref, kbuf, vbuf, lbuf, sem, m_sc, l_sc, acc_sc, *, ppc, nch, page):
    b = pl.program_id(0)
    nb = pl.num_programs(0)
    nrow = q_ref.shape[1]
    nt = nrow // FOX_HEADS

    def copies(bb, c, slot):
        out = []
        for p in range(ppc):
            pid = pt_ref[bb, c * ppc + p]
            out.append(pltpu.make_async_copy(kc_hbm.at[pid], kbuf.at[slot, p], sem.at[0, slot]))
            out.append(pltpu.make_async_copy(vc_hbm.at[pid], vbuf.at[slot, p], sem.at[1, slot]))
            out.append(pltpu.make_async_copy(lc_hbm.at[pid], lbuf.at[slot, p], sem.at[2, slot]))
        return out

    def start(bb, c, slot):
        for cp in copies(bb, c, slot):
            cp.start()

    def wait(bb, c, slot):
        for cp in copies(bb, c, slot):
            cp.wait()

    @pl.when(b == 0)
    def _():
        start(0, 0, 0)

    q = q_ref[0]
    tri = _tri_upper(page)
    m_sc[...] = jnp.full(m_sc.shape, NEG_INF, F32)
    l_sc[...] = jnp.zeros(l_sc.shape, F32)
    acc_sc[...] = jnp.zeros(acc_sc.shape, F32)

    def update(s, pv_fn):
        m_old = m_sc[...]
        m_new = jnp.maximum(m_old, jnp.max(s, axis=-1, keepdims=True))
        alpha = jnp.exp2(m_old - m_new)
        p = jnp.exp2(s - m_new)
        l_sc[...] = alpha * l_sc[...] + jnp.sum(p, axis=-1, keepdims=True)
        acc_sc[...] = alpha * acc_sc[...] + pv_fn(p)
        m_sc[...] = m_new

    carry = jnp.zeros((FOX_HEADS, 1), F32)
    for c in range(nch):
        slot = (b * nch + c) % 2
        if c + 1 < nch:
            start(b, c + 1, 1 - slot)
        else:
            @pl.when(b + 1 < nb)
            def _():
                start(b + 1, 0, 1 - slot)
        wait(b, c, slot)
        w = _cumsum_lanes(lbuf[slot].reshape(ppc * FOX_HEADS, page), tri)
        cs, ss = [], []
        for p in range(ppc):
            wp = w[p * FOX_HEADS:(p + 1) * FOX_HEADS]
            cs.append(wp + carry)
            carry = carry + wp[:, page - 1:page]
            ss.append(_dot(q, kbuf[slot, p]))
        ck = jnp.concatenate(cs, axis=1) * LOG2E
        s = jnp.concatenate(ss, axis=1) - jnp.concatenate([ck] * nt, axis=0)

        def pv_pages(pr, slot=slot):
            out = _dot_nt(pr[:, 0:page], vbuf[slot, 0])
            for p in range(1, ppc):
                out += _dot_nt(pr[:, p * page:(p + 1) * page], vbuf[slot, p])
            return out

        update(s, pv_pages)

    lfn = lfn_ref[0]
    cn = (carry + _cumsum_lanes(lfn, _tri_upper(LANES))) * LOG2E
    sn = _dot_nt(q, kn_ref[0]) - jnp.concatenate([cn[:, 0:SUBLANES]] * nt, axis=0)
    r = lax.broadcasted_iota(jnp.int32, (nrow, SUBLANES), 0) // FOX_HEADS
    j = lax.broadcasted_iota(jnp.int32, (nrow, SUBLANES), 1)
    sn = jnp.where(j <= r, sn, NEG_INF)
    update(sn, lambda pr: _dot(pr, vn_ref[0]))

    res = acc_sc[...] / l_sc[...]
    lane_head = lax.broadcasted_iota(jnp.int32, (FOX_HEADS, FOX_WIDTH), 1) // FOX_HEAD_DIM
    sub_head = lax.broadcasted_iota(jnp.int32, (FOX_HEADS, FOX_WIDTH), 0)
    rows = []
    for t in range(nt):
        blk = res[t * FOX_HEADS:(t + 1) * FOX_HEADS]
        rows.append(jnp.sum(jnp.where(lane_head == sub_head, blk, 0.0), axis=0, keepdims=True))
    o_ref[0] = jnp.concatenate(rows, axis=0).astype(o_ref.dtype)


def _fox_sample(q, k_new, v_new, lft_new, cache_k, cache_v, cache_lf, page_table):
    nb, nt, _ = q.shape
    n_pool, page = cache_k.shape[0], cache_k.shape[1]
    n_pages = page_table.shape[1]
    ppc = min(FOX_PAGES_PER_CHUNK, n_pages)
    nch = n_pages // ppc
    nrow = nt * FOX_HEADS
    qf = q.astype(F32)
    head_of_lane = jnp.arange(FOX_WIDTH) // FOX_HEAD_DIM
    qrows = jnp.where(head_of_lane[None, None, None, :] == jnp.arange(FOX_HEADS)[None, None, :, None],
                      qf[:, :, None, :], 0.0).reshape(nb, nrow, FOX_WIDTH)
    pad = SUBLANES - nt
    kn = jnp.pad(k_new, ((0, 0), (0, pad), (0, 0)))
    vn = jnp.pad(v_new, ((0, 0), (0, pad), (0, 0)))
    lfn = jnp.pad(lft_new.reshape(FOX_HEADS, nb, nt).transpose(1, 0, 2),
                  ((0, 0), (0, 0), (0, LANES - nt)))
    kc2 = cache_k.transpose(0, 2, 3, 1).reshape(n_pool, FOX_WIDTH, page)
    vc2 = cache_v.transpose(0, 2, 3, 1).reshape(n_pool, FOX_WIDTH, page)
    lc2 = cache_lf.transpose(0, 2, 1)
    per_b = lambda shape: pl.BlockSpec((1,) + shape, lambda b, pt: (b, 0, 0))
    grid_spec = pltpu.PrefetchScalarGridSpec(
        num_scalar_prefetch=1,
        grid=(nb,),
        in_specs=[per_b((nrow, FOX_WIDTH)), per_b((SUBLANES, FOX_WIDTH)), per_b((SUBLANES, FOX_WIDTH)),
                  per_b((FOX_HEADS, LANES)),
                  pl.BlockSpec(memory_space=pl.ANY), pl.BlockSpec(memory_space=pl.ANY),
                  pl.BlockSpec(memory_space=pl.ANY)],
        out_specs=per_b((nt, FOX_WIDTH)),
        scratch_shapes=[
            pltpu.VMEM((2, ppc, FOX_WIDTH, page), F32),
            pltpu.VMEM((2, ppc, FOX_WIDTH, page), F32),
            pltpu.VMEM((2, ppc, FOX_HEADS, page), F32),
            pltpu.SemaphoreType.DMA((3, 2)),
            pltpu.VMEM((nrow, 1), F32), pltpu.VMEM((nrow, 1), F32),
            pltpu.VMEM((nrow, FOX_WIDTH), F32),
        ],
    )
    return pl.pallas_call(
        functools.partial(_fox_sample_kernel, ppc=ppc, nch=nch, page=page),
        grid_spec=grid_spec,
        out_shape=jax.ShapeDtypeStruct((nb, nt, FOX_WIDTH), BF16),
        compiler_params=_cparams(("arbitrary",)),
        name="fox_sample",
    )(page_table, qrows, kn, vn, lfn, kc2, vc2, lc2)


def _s5_kernel(u_ref, h0_ref, ar_ref, ai_ref, bm_ref, cm_ref, d_ref, wg_ref, bg_ref,
               o_ref, hf_ref, s_sc, h_sc, *, tt, nslab):
    ti = pl.program_id(1)
    ns = S5_NSTATE
    half = MXU_DIM
    hs = ns // 2

    @pl.when(ti == 0)
    def _():
        h_sc[...] = h0_ref[...]

    u = u_ref[...]
    ub = u.astype(BF16)
    for h in range(2):
        uh = ub[:, h * half:(h + 1) * half]
        s_sc[:, h * hs:(h + 1) * hs] = _dot(uh, bm_ref[2 * h])
        s_sc[:, ns + h * hs:ns + (h + 1) * hs] = _dot(uh, bm_ref[2 * h + 1])

    sw = ns // nslab
    for sl in range(nslab):
        lo = sl * sw
        ar = jnp.broadcast_to(ar_ref[:, lo:lo + sw], (SUBLANES, sw))
        ai = jnp.broadcast_to(ai_ref[:, lo:lo + sw], (SUBLANES, sw))

        def body(t, carry):
            hr, hi = carry
            r0 = pl.multiple_of(t * SUBLANES, SUBLANES)
            nr = ar * hr - ai * hi + s_sc[pl.ds(r0, SUBLANES), lo:lo + sw]
            ni = ar * hi + ai * hr + s_sc[pl.ds(r0, SUBLANES), ns + lo:ns + lo + sw]
            s_sc[pl.ds(r0, SUBLANES), lo:lo + sw] = nr
            s_sc[pl.ds(r0, SUBLANES), ns + lo:ns + lo + sw] = ni
            return nr, ni

        hr, hi = lax.fori_loop(0, tt, body, (h_sc[:, lo:lo + sw], h_sc[:, ns + lo:ns + lo + sw]))
        h_sc[:, lo:lo + sw] = hr
        h_sc[:, ns + lo:ns + lo + sw] = hi

    ys = []
    for h in range(2):
        yr = _dot(s_sc[:, h * hs:(h + 1) * hs].astype(BF16), cm_ref[2 * h])
        yi = _dot(s_sc[:, ns + h * hs:ns + (h + 1) * hs].astype(BF16), cm_ref[2 * h + 1])
        ys.append(yr + yi)
    y = jnp.concatenate(ys, axis=1) + d_ref[...] * u
    zg = 0.5 * y * (1.0 + jnp.tanh(math.sqrt(2.0 / math.pi) * (y + 0.044715 * (y * y * y))))
    gl = _dot(zg.astype(BF16), wg_ref[...]) + bg_ref[...]
    o_ref[...] = (zg * _sigmoid(gl)).astype(o_ref.dtype)

    @pl.when(ti == pl.num_programs(1) - 1)
    def _():
        hf_ref[...] = h_sc[...]


def _s5(u_tb, h0, ar, ai, bmat, cmat, d, wglu, bglu, nbg, t):
    tt = min(S5_TT, t)
    nti = t // tt
    rows = tt * SUBLANES
    full = lambda a: pl.BlockSpec(a.shape, lambda g, i: (0,) * a.ndim)
    tok = pl.BlockSpec((rows, S5_WIDTH), lambda g, i: (g * nti + i, 0))
    st = pl.BlockSpec((SUBLANES, 2 * S5_NSTATE), lambda g, i: (g, 0))
    return pl.pallas_call(
        functools.partial(_s5_kernel, tt=tt, nslab=4),
        grid=(nbg, nti),
        in_specs=[tok, st, full(ar), full(ai), full(bmat), full(cmat), full(d), full(wglu), full(bglu)],
        out_specs=(tok, st),
        out_shape=(jax.ShapeDtypeStruct(u_tb.shape, BF16),
                   jax.ShapeDtypeStruct(h0.shape, F32)),
        scratch_shapes=[pltpu.VMEM((rows, 2 * S5_NSTATE), F32),
                        pltpu.VMEM((SUBLANES, 2 * S5_NSTATE), F32)],
        compiler_params=_cparams(("parallel", "arbitrary")),
        name="s5",
    )(u_tb, h0, ar, ai, bmat, cmat, d, wglu, bglu)


def _memkv_kernel(m_ref, g_ref, w_ref, k32_ref, v32_ref, kb_ref, vb_ref):
    xn = _rms(m_ref[...], g_ref[...]).astype(BF16)
    k = _dot(xn, w_ref[:, 0:MEM_WIDTH])
    v = _dot(xn, w_ref[:, MEM_WIDTH:2 * MEM_WIDTH])
    k32_ref[...] = k
    v32_ref[...] = v
    kb_ref[...] = k.astype(BF16)
    vb_ref[...] = v.astype(BF16)


def _memkv(mem2d, g_mem, w_kv):
    n, d = mem2d.shape
    tm = min(PROJ_TM, n)
    g2 = g_mem.reshape(1, d)
    tok = lambda width: pl.BlockSpec((tm, width), lambda i: (i, 0))
    full = lambda a: pl.BlockSpec(a.shape, lambda i: (0,) * a.ndim)
    sh = lambda dt: jax.ShapeDtypeStruct((n, MEM_WIDTH), dt)
    return pl.pallas_call(
        _memkv_kernel,
        grid=(n // tm,),
        in_specs=[tok(d), full(g2), full(w_kv)],
        out_specs=(tok(MEM_WIDTH),) * 4,
        out_shape=(sh(F32), sh(F32), sh(BF16), sh(BF16)),
        compiler_params=_cparams(("parallel",)),
        name="mem_kv",
    )(mem2d, g2, w_kv)


def _memattn_kernel(q_ref, k_ref, v_ref, o_ref):
    q = q_ref[0]
    outs = []
    for h in range(MEM_HEADS):
        sl = slice(h * MEM_HEAD_DIM, (h + 1) * MEM_HEAD_DIM)
        kh = k_ref[0, :, sl].astype(BF16)
        vh = v_ref[0, :, sl].astype(BF16)
        s = _dot_nt(q[:, sl], kh)
        p = jnp.exp(s - jnp.max(s, axis=-1, keepdims=True))
        o = _dot(p.astype(BF16), vh)
        outs.append(o / jnp.sum(p, axis=-1, keepdims=True))
    o_ref[0] = jnp.concatenate(outs, axis=1).astype(o_ref.dtype)


def _memattn(qm, mk, mv):
    nb, t, _ = qm.shape
    n_mem = mk.shape[1]
    tq = min(MEMATTN_TQ, t)
    return pl.pallas_call(
        _memattn_kernel,
        grid=(nb, t // tq),
        in_specs=[pl.BlockSpec((1, tq, MEM_WIDTH), lambda b, i: (b, i, 0)),
                  pl.BlockSpec((1, n_mem, MEM_WIDTH), lambda b, i: (b, 0, 0)),
                  pl.BlockSpec((1, n_mem, MEM_WIDTH), lambda b, i: (b, 0, 0))],
        out_specs=pl.BlockSpec((1, tq, MEM_WIDTH), lambda b, i: (b, i, 0)),
        out_shape=jax.ShapeDtypeStruct(qm.shape, BF16),
        compiler_params=_cparams(("parallel", "parallel")),
        name="mem_attn",
    )(qm, mk, mv)


def _merge_kernel(x_ref, g_ref, wg_ref, of_ref, os_ref, om_ref, wf_ref, ws_ref, wm_ref, wo_ref, o_ref):
    x = x_ref[...]
    d = x.shape[1]
    xn = _rms(x, g_ref[...]).astype(BF16)
    merged = _sigmoid(_dot(xn, wg_ref[:, 0:d])) * _dot(of_ref[...], wf_ref[...])
    merged += _sigmoid(_dot(xn, wg_ref[:, d:2 * d])) * _dot(os_ref[...], ws_ref[...])
    merged += _sigmoid(_dot(xn, wg_ref[:, 2 * d:3 * d])) * _dot(om_ref[...], wm_ref[...])
    o_ref[...] = x + _dot(merged.astype(BF16), wo_ref[...])


def _merge(x2d, g_mix, w_gate, o_fox, o_s5, o_mem, w_f, w_s, w_m, w_out):
    n, d = x2d.shape
    tm = min(MERGE_TM, n)
    g2 = g_mix.reshape(1, d)
    tok = lambda width: pl.BlockSpec((tm, width), lambda i: (i, 0))
    full = lambda a: pl.BlockSpec(a.shape, lambda i: (0,) * a.ndim)
    return pl.pallas_call(
        _merge_kernel,
        grid=(n // tm,),
        in_specs=[tok(d), full(g2), full(w_gate), tok(FOX_WIDTH), tok(S5_WIDTH), tok(MEM_WIDTH),
                  full(w_f), full(w_s), full(w_m), full(w_out)],
        out_specs=tok(d),
        out_shape=jax.ShapeDtypeStruct((n, d), F32),
        compiler_params=_cparams(("parallel",)),
        name="merge",
    )(x2d, g2, w_gate, o_fox, o_s5, o_mem, w_f, w_s, w_m, w_out)


def _first_argmax(vals):
    m = vals[0]
    for v in vals[1:]:
        m = jnp.maximum(m, v)
    idx = jnp.full(m.shape, len(vals) - 1, jnp.int32)
    for i in range(len(vals) - 2, -1, -1):
        idx = jnp.where(vals[i] == m, i, idx)
    return idx, m


def _moe_kernel(x_ref, g_ref, wr_ref, wrl_ref, br_ref, wg_ref, wu_ref, wd_ref, gf_ref, o_ref,
                h_sc, r_sc, acc_sc):
    j = pl.program_id(1)
    ng, ne = N_EXPERT_GROUPS, EXPERTS_PER_GROUP

    @pl.when(j == 0)
    def _():
        hf = _rms(x_ref[...], g_ref[...])
        h = hf.astype(BF16)
        h_sc[...] = h
        h_lo = (hf - h.astype(F32)).astype(BF16)
        lg = _dot(h, wr_ref[...]) + _dot(h_lo, wr_ref[...]) + _dot(h, wrl_ref[...]) + br_ref[...]
        lt = lg.T
        col = lambda i: lt[i:i + 1, :]
        gl = [col(i) for i in range(ng)]
        gidx, gm = _first_argmax(gl)
        den = jnp.exp(gl[0] - gm)
        for v in gl[1:]:
            den = den + jnp.exp(v - gm)
        p_sel = 1.0 / den
        le = []
        for e in range(ne):
            v = col(ng + e)
            for g in range(1, ng):
                v = jnp.where(gidx == g, col(ng + g * ne + e), v)
            le.append(v)
        i1, v1 = _first_argmax(le)
        rest = [jnp.where(i1 == e, -jnp.inf, le[e]) for e in range(ne)]
        i2, v2 = _first_argmax(rest)
        t = jnp.exp(v2 - v1)
        w1 = 1.0 / (1.0 + t)
        w2 = t / (1.0 + t)
        rows = [jnp.where(i1 == e, w1, 0.0) + jnp.where(i2 == e, w2, 0.0) for e in range(ne)]
        rows += [p_sel, gidx.astype(F32)]
        rt = jnp.concatenate(rows + [jnp.zeros((LANES - len(rows), lt.shape[1]), F32)], axis=0)
        r_sc[...] = rt.T
        acc_sc[...] = jnp.zeros(acc_sc.shape, F32)

    h = h_sc[...]
    r = r_sc[...]
    scale = jnp.where(r[:, ne + 1:ne + 2] == j.astype(F32), r[:, ne:ne + 1], 0.0)
    down = None
    for e in range(ne):
        sl = slice(e * D_EXPERT, (e + 1) * D_EXPERT)
        hg = _dot(h, wg_ref[:, sl])
        hu = _dot(h, wu_ref[:, sl])
        act = (hg * _sigmoid(hg) * hu * (scale * r[:, e:e + 1])).astype(BF16)
        d = _dot(act, wd_ref[sl, :])
        down = d if down is None else down + d
    acc_sc[...] += down

    @pl.when(j == ng - 1)
    def _():
        o_ref[...] = _rms(x_ref[...] + acc_sc[...], gf_ref[...])


def _moe(x2d, g_ffn, w_r, w_r_lo, b_r, w_g, w_u, w_d, g_final):
    n, d = x2d.shape
    tm = min(MOE_TM, n)
    gw = EXPERTS_PER_GROUP * D_EXPERT
    g2 = g_ffn.reshape(1, d)
    gf2 = g_final.reshape(1, d)
    tok = pl.BlockSpec((tm, d), lambda i, j: (i, 0))
    full = lambda a: pl.BlockSpec(a.shape, lambda i, j: (0,) * a.ndim)
    return pl.pallas_call(
        _moe_kernel,
        grid=(n // tm, N_EXPERT_GROUPS),
        in_specs=[tok, full(g2), full(w_r), full(w_r_lo), full(b_r),
                  pl.BlockSpec((d, gw), lambda i, j: (0, j)),
                  pl.BlockSpec((d, gw), lambda i, j: (0, j)),
                  pl.BlockSpec((gw, d), lambda i, j: (j, 0)),
                  full(gf2)],
        out_specs=tok,
        out_shape=jax.ShapeDtypeStruct((n, d), F32),
        scratch_shapes=[pltpu.VMEM((tm, d), BF16), pltpu.VMEM((tm, LANES), F32),
                        pltpu.VMEM((tm, d), F32)],
        compiler_params=_cparams(("parallel", "arbitrary")),
        name="moe",
    )(x2d, g2, w_r, w_r_lo, b_r, w_g, w_u, w_d, gf2)


def _block_diag(blocks):
    nblk, r, c = blocks.shape
    eye = jnp.eye(nblk, dtype=blocks.dtype)
    return (eye[:, None, :, None] * blocks[:, :, None, :]).reshape(nblk * r, nblk * c)


def _prep_params(g_mix, w_in, b_forget, s5_a_re, s5_a_im, s5_log_dt, s5_b_re, s5_b_im, s5_c_re, s5_c_im,
                 s5_d, s5_w_glu, s5_b_glu, w_br_fox, w_br_s5, w_br_mem, w_out,
                 w_router_group, b_router_group, w_router_expert, b_router_expert,
                 w_exp_gate, w_exp_up, w_exp_down):
    d = w_in.shape[0]
    o3 = 3 * FOX_WIDTH
    o4 = o3 + FOX_HEADS
    o6 = o4 + S5_WIDTH + MEM_WIDTH
    p = {}
    o1, o2 = FOX_WIDTH, 2 * FOX_WIDTH
    p['w_a'] = jnp.concatenate([w_in[:, :o1], w_in[:, o2:o3], w_in[:, o4:o6]], axis=1).astype(BF16)
    p['w_t'] = w_in[:, o1:o3].T.astype(BF16)
    wft = jnp.zeros((2 * SUBLANES, d), F32).at[:FOX_HEADS].set(w_in[:, o3:o4].T)
    p['wft'] = wft.astype(BF16)
    p['w_gate'] = w_in[:, o6:].astype(BF16)
    a_re = jnp.minimum(s5_a_re.astype(F32), A_RE_MAX)
    a_im = s5_a_im.astype(F32)
    dt = jnp.exp(s5_log_dt.astype(F32))[:, None]
    mag = jnp.exp(dt * a_re)
    ang = dt * a_im
    ab_re = mag * jnp.cos(ang)
    ab_im = mag * jnp.sin(ang)
    den = a_re * a_re + a_im * a_im
    n_re = ab_re - 1.0
    n_im = ab_im
    s_re = (n_re * a_re + n_im * a_im) / den
    s_im = (n_im * a_re - n_re * a_im) / den
    bb_re = s_re[..., None] * s5_b_re - s_im[..., None] * s5_b_im
    bb_im = s_re[..., None] * s5_b_im + s_im[..., None] * s5_b_re
    p['ar'] = ab_re.reshape(1, S5_NSTATE)
    p['ai'] = ab_im.reshape(1, S5_NSTATE)
    gh = S5_GROUPS // 2
    bm, cm = [], []
    for h in range(2):
        gs = slice(h * gh, (h + 1) * gh)
        bm.append(_block_diag(bb_re[gs].transpose(0, 2, 1)))
        bm.append(_block_diag(bb_im[gs].transpose(0, 2, 1)))
        cm.append(_block_diag(s5_c_re[gs].transpose(0, 2, 1)))
        cm.append(_block_diag(-s5_c_im[gs].transpose(0, 2, 1)))
    p['bmat'] = jnp.stack(bm).astype(BF16)
    p['cmat'] = jnp.stack(cm).astype(BF16)
    p['d'] = s5_d.reshape(1, S5_WIDTH).astype(F32)
    p['wglu'] = s5_w_glu.astype(BF16)
    p['bglu'] = s5_b_glu.reshape(1, S5_WIDTH).astype(F32)
    p['w_f'] = w_br_fox.astype(BF16)
    p['w_s'] = w_br_s5.astype(BF16)
    p['w_m'] = w_br_mem.astype(BF16)
    p['w_out'] = w_out.astype(BF16)
    nr = N_EXPERT_GROUPS * (1 + EXPERTS_PER_GROUP)
    w_r = jnp.concatenate([w_router_group, w_router_expert.reshape(d, -1)], axis=1)
    w_r = jnp.pad(w_r, ((0, 0), (0, LANES - nr))).astype(F32)
    p['w_r'] = w_r.astype(BF16)
    p['w_r_lo'] = (w_r - p['w_r'].astype(F32)).astype(BF16)
    b_r = jnp.concatenate([b_router_group, b_router_expert.reshape(-1)])
    p['b_r'] = jnp.pad(b_r, (0, LANES - nr)).reshape(1, LANES).astype(F32)
    p['w_eg'] = w_exp_gate.astype(BF16)
    p['w_eu'] = w_exp_up.astype(BF16)
    p['w_ed'] = w_exp_down.astype(BF16)
    p['g_mix'] = g_mix
    p['b_forget'] = b_forget
    return p


def _to_time_major(a, nbg, t):
    w = a.shape[-1]
    return a.reshape(nbg, SUBLANES, t, w).transpose(0, 2, 1, 3).reshape(nbg * t * SUBLANES, w)


def _from_time_major(a, nbg, t):
    w = a.shape[-1]
    return a.reshape(nbg, t, SUBLANES, w).transpose(0, 2, 1, 3).reshape(nbg * SUBLANES, t, w)


def _layer(x, p, fox_fn, h0_re, h0_im, mem_k, mem_v, g_ffn, g_final):
    nb, t, d = x.shape
    n = nb * t
    x2d = x.reshape(n, d)
    pnb, pt = (nb, t) if t % LANES == 0 else (1, n)
    q, kt32, ktb, vt32, vb, u, qm, lft = _proj(x2d, pnb, pt, p['g_mix'], p['w_a'], p['w_t'], p['wft'],
                                               p['b_forget'])
    r3 = lambda a: a.reshape(nb, t, a.shape[-1])
    heads_last = lambda a: (a.reshape(pnb, FOX_HEADS, FOX_HEAD_DIM, pt).transpose(0, 3, 1, 2)
                            .reshape(nb, t, FOX_HEADS, FOX_HEAD_DIM))
    k_out = heads_last(kt32)
    v_out = heads_last(vt32)
    o_fox = fox_fn(r3(q), ktb, r3(vb), k_out.reshape(nb, t, FOX_WIDTH), v_out.reshape(nb, t, FOX_WIDTH), lft)
    nbg = nb // SUBLANES
    h0 = jnp.concatenate([h0_re.reshape(nb, S5_NSTATE), h0_im.reshape(nb, S5_NSTATE)], axis=1).astype(F32)
    o_s5_tb, hf = _s5(_to_time_major(r3(u), nbg, t), h0, p['ar'], p['ai'], p['bmat'], p['cmat'],
                      p['d'], p['wglu'], p['bglu'], nbg, t)
    o_s5 = _from_time_major(o_s5_tb, nbg, t)
    o_mem = _memattn(r3(qm), mem_k, mem_v)
    x1 = _merge(x2d, p['g_mix'], p['w_gate'], o_fox.reshape(n, FOX_WIDTH), o_s5.reshape(n, S5_WIDTH),
                o_mem.reshape(n, MEM_WIDTH), p['w_f'], p['w_s'], p['w_m'], p['w_out'])
    y = _moe(x1, g_ffn, p['w_r'], p['w_r_lo'], p['b_r'], p['w_eg'], p['w_eu'], p['w_ed'], g_final)
    lf_out = lft.reshape(FOX_HEADS, nb, t).transpose(1, 2, 0)
    h_re = hf[:, :S5_NSTATE].reshape(nb, S5_GROUPS, S5_STATE)
    h_im = hf[:, S5_NSTATE:].reshape(nb, S5_GROUPS, S5_STATE)
    return y.reshape(nb, t, d), k_out, v_out, lf_out, h_re, h_im


def kernel(x_prompt, x_sample, cache_fox_k, cache_fox_v, cache_fox_logf, cache_mem_k, cache_mem_v, state_s5_re, state_s5_im, page_table, mem_prompt, g_mix, w_in, b_forget, s5_a_re, s5_a_im, s5_log_dt, s5_b_re, s5_b_im, s5_c_re, s5_c_im, s5_d, s5_w_glu, s5_b_glu, g_mem, w_mem_kv, w_br_fox, w_br_s5, w_br_mem, w_out, g_ffn, w_router_group, b_router_group, w_router_expert, b_router_expert, w_exp_gate, w_exp_up, w_exp_down, g_final):
    depth = w_in.shape[0]
    assert depth == 1, "single-layer step"
    l = 0
    bp, sp, d = x_prompt.shape
    db, ds, _ = x_sample.shape
    n_mem = mem_prompt.shape[1]
    p = _prep_params(g_mix[l], w_in[l], b_forget[l], s5_a_re[l], s5_a_im[l], s5_log_dt[l], s5_b_re[l],
                     s5_b_im[l], s5_c_re[l], s5_c_im[l], s5_d[l], s5_w_glu[l], s5_b_glu[l],
                     w_br_fox[l], w_br_s5[l], w_br_mem[l], w_out[l],
                     w_router_group[l], b_router_group[l], w_router_expert[l], b_router_expert[l],
                     w_exp_gate[l], w_exp_up[l], w_exp_down[l])

    mk32, mv32, mkb, mvb = _memkv(mem_prompt.reshape(bp * n_mem, d), g_mem[l], w_mem_kv[l].astype(BF16))
    mem3 = lambda a: a.reshape(bp, n_mem, MEM_WIDTH)

    def fox_p(q, ktb, vb, k32, v32, lft):
        c = _cumsum(lft, bp, sp)
        c = c.reshape(FOX_WIDTH // MXU_DIM, HEADS_PER_SLAB, bp * sp)
        return _fox_prompt(q, ktb, vb, c, bp, sp)

    zero = jnp.zeros((bp, S5_GROUPS, S5_STATE), F32)
    yp, kp, vp, lfp, hr_p, hi_p = _layer(x_prompt, p, fox_p, zero, zero, mem3(mkb), mem3(mvb),
                                         g_ffn[l], g_final)

    def fox_s(q, ktb, vb, k32, v32, lft):
        return _fox_sample(q, k32, v32, lft, cache_fox_k[l], cache_fox_v[l], cache_fox_logf[l], page_table)

    ys, ks, vs, lfs, hr_s, hi_s = _layer(x_sample, p, fox_s, state_s5_re[l], state_s5_im[l],
                                         cache_mem_k[l].reshape(db, n_mem, MEM_WIDTH),
                                         cache_mem_v[l].reshape(db, n_mem, MEM_WIDTH),
                                         g_ffn[l], g_final)
    e = lambda a: a[None]
    mk4 = mk32.reshape(bp, n_mem, MEM_HEADS, MEM_HEAD_DIM)
    mv4 = mv32.reshape(bp, n_mem, MEM_HEADS, MEM_HEAD_DIM)
    return (yp, ys, e(kp), e(vp), e(lfp), e(ks), e(vs), e(lfs), e(mk4), e(mv4),
            e(hr_p), e(hi_p), e(hr_s), e(hi_s))
```

```python
import functools
import math

import jax
import jax.numpy as jnp
from jax import lax
from jax.experimental import pallas as pl
from jax.experimental.pallas import tpu as pltpu

F32 = jnp.float32
BF16 = jnp.bfloat16

FOX_HEADS = 8
FOX_HEAD_DIM = 64
FOX_WIDTH = FOX_HEADS * FOX_HEAD_DIM
FOX_SCALE = FOX_HEAD_DIM ** -0.5
S5_GROUP = 16
S5_WIDTH = 512
S5_GROUPS = S5_WIDTH // S5_GROUP
S5_STATE = 64
S5_NSTATE = S5_GROUPS * S5_STATE
A_RE_MAX = -1e-4
MEM_HEADS = 4
MEM_HEAD_DIM = 128
MEM_WIDTH = MEM_HEADS * MEM_HEAD_DIM
MEM_SCALE = MEM_HEAD_DIM ** -0.5
N_BRANCH = 3
N_EXPERT_GROUPS = 4
EXPERTS_PER_GROUP = 4
D_EXPERT = 256
EPS = 1e-6
NEG_INF = -1e30
LOG2E = math.log2(math.e)

LANES = 128
SUBLANES = 8
MXU_DIM = 256
VMEM_LIMIT = 56 * 1024 * 1024

PROJ_TM = 512
CUMSUM_BLK = 512
FOX_TQ = 512
FOX_ROW_CHUNK = 16
FOX_PAGES_PER_CHUNK = 16
S5_TT = 128
MEMATTN_TQ = 1024
MERGE_TM = 512
MOE_TM = 1024


def _cparams(sem):
    return pltpu.CompilerParams(dimension_semantics=sem, vmem_limit_bytes=VMEM_LIMIT)


def _rms(x, g):
    return x * lax.rsqrt(jnp.mean(x * x, axis=-1, keepdims=True) + EPS) * g


def _dot(a, b):
    return jnp.dot(a, b, preferred_element_type=F32)


def _dot_nt(a, b):
    return lax.dot_general(a, b, (((1,), (1,)), ((), ())), preferred_element_type=F32)


def _split3(x):
    hi = x.astype(BF16)
    r = x - hi.astype(F32)
    mid = r.astype(BF16)
    lo = (r - mid.astype(F32)).astype(BF16)
    return hi, mid, lo


def _sigmoid(x):
    return 1.0 / (1.0 + jnp.exp(-x))


def _proj_kernel(x_ref, g_ref, w_ref, wt_ref, wft_ref, bf_ref,
                 q_ref, kt32_ref, ktb_ref, vt32_ref, vb_ref, u_ref, qm_ref, lf_ref):
    xn = _rms(x_ref[...], g_ref[...]).astype(BF16)
    w = FOX_WIDTH
    q_ref[...] = (_dot(xn, w_ref[:, 0:w]) * (FOX_SCALE * LOG2E)).astype(BF16)
    vb_ref[...] = _dot(xn, w_ref[:, w:2 * w]).astype(BF16)
    u_ref[...] = _dot(xn, w_ref[:, 2 * w:2 * w + S5_WIDTH])
    o = 2 * w + S5_WIDTH
    qm_ref[...] = (_dot(xn, w_ref[:, o:o + MEM_WIDTH]) * MEM_SCALE).astype(BF16)
    kt = _dot_nt(wt_ref[0:w, :], xn)
    kt32_ref[0] = kt
    ktb_ref[0] = kt.astype(BF16)
    vt32_ref[0] = _dot_nt(wt_ref[w:2 * w, :], xn)
    zf = _dot_nt(wft_ref[...], xn)[0:FOX_HEADS, :] + bf_ref[...]
    lf_ref[...] = jnp.minimum(zf, 0.0) - jnp.log1p(jnp.exp(-jnp.abs(zf)))


def _proj(x2d, nb, t, g_mix, w_a, w_t, wft, b_forget):
    n, d = x2d.shape
    tm = min(PROJ_TM, t)
    nti = t // tm
    tok = lambda width: pl.BlockSpec((tm, width), lambda i: (i, 0))
    full = lambda a: pl.BlockSpec(a.shape, lambda i: (0,) * a.ndim)
    tr = pl.BlockSpec((1, FOX_WIDTH, tm), lambda i: (i // nti, 0, i % nti))
    trs = lambda dt: jax.ShapeDtypeStruct((nb, FOX_WIDTH, t), dt)
    out_shape = (
        jax.ShapeDtypeStruct((n, FOX_WIDTH), BF16),
        trs(F32),
        trs(BF16),
        trs(F32),
        jax.ShapeDtypeStruct((n, FOX_WIDTH), BF16),
        jax.ShapeDtypeStruct((n, S5_WIDTH), F32),
        jax.ShapeDtypeStruct((n, MEM_WIDTH), BF16),
        jax.ShapeDtypeStruct((FOX_HEADS, n), F32),
    )
    g2 = g_mix.reshape(1, d)
    bf2 = b_forget.reshape(FOX_HEADS, 1)
    return pl.pallas_call(
        _proj_kernel,
        grid=(n // tm,),
        in_specs=[tok(d), full(g2), full(w_a), full(w_t), full(wft), full(bf2)],
        out_specs=(tok(FOX_WIDTH), tr, tr, tr, tok(FOX_WIDTH),
                   tok(S5_WIDTH), tok(MEM_WIDTH), pl.BlockSpec((FOX_HEADS, tm), lambda i: (0, i))),
        out_shape=out_shape,
        compiler_params=_cparams(("parallel",)),
        name="proj",
    )(x2d, g2, w_a, w_t, wft, bf2)


def _tri_upper(n):
    r = lax.broadcasted_iota(jnp.int32, (n, n), 0)
    c = lax.broadcasted_iota(jnp.int32, (n, n), 1)
    return jnp.where(r <= c, 1.0, 0.0).astype(BF16)


def _cumsum_lanes(x, tri):
    hi, mid, lo = _split3(x)
    return _dot(hi, tri) + _dot(mid, tri) + _dot(lo, tri)


def _cumsum_kernel(lf_ref, c_ref, *, blk):
    t = lf_ref.shape[1]
    tri = _tri_upper(blk)

    def body(j, carry):
        off = pl.multiple_of(j * blk, blk)
        w = _cumsum_lanes(lf_ref[:, pl.ds(off, blk)], tri) + carry
        c_ref[:, pl.ds(off, blk)] = w
        return w[:, blk - 1:blk]

    lax.fori_loop(0, t // blk, body, jnp.zeros((FOX_HEADS, 1), F32))


def _cumsum(lft, nb, t):
    blk = min(CUMSUM_BLK, t)
    spec = pl.BlockSpec((FOX_HEADS, t), lambda b: (0, b))
    return pl.pallas_call(
        functools.partial(_cumsum_kernel, blk=blk),
        grid=(nb,),
        in_specs=[spec],
        out_specs=spec,
        out_shape=jax.ShapeDtypeStruct(lft.shape, F32),
        compiler_params=_cparams(("parallel",)),
        name="logf_cumsum",
    )(lft)


HEADS_PER_SLAB = MXU_DIM // FOX_HEAD_DIM


def _fox_kernel(q_ref, kt_ref, v_ref, c_ref, o_ref, qs_sc, p0_sc, p1_sc, m_sc, acc_sc, *, tq):
    qi = pl.program_id(2)
    nh = HEADS_PER_SLAB
    q = q_ref[0]
    lane = lax.broadcasted_iota(jnp.int32, (1, MXU_DIM), 1)
    for h in range(nh):
        qs_sc[h * tq:(h + 1) * tq, :] = jnp.where((lane // FOX_HEAD_DIM) == h, q, jnp.zeros_like(q))
    q0 = pl.multiple_of(qi * tq, tq)
    cref = c_ref[0, :, pl.ds(q0, LANES)][:, 0:1]
    m_sc[...] = jnp.full(m_sc.shape, NEG_INF, F32)
    acc_sc[...] = jnp.zeros(acc_sc.shape, F32)
    p1_sc[...] = jnp.zeros(p1_sc.shape, BF16)

    wide = 2 * tq
    lane_head = lax.broadcasted_iota(jnp.int32, (1, LANES), 1) // FOX_HEAD_DIM

    def values(koff, width, p_in):
        v = v_ref[0, pl.ds(koff, width), :]
        out = []
        for h in range(nh):
            slab = v[:, (h // 2) * LANES:(h // 2 + 1) * LANES]
            vh = jnp.where(lane_head == h % 2, slab, jnp.ones_like(slab))
            out.append(_dot(p_in[h * tq:(h + 1) * tq, 0:width], vh))
        return out

    def block(koff, width, p_in, p_out, shift):
        koff = pl.multiple_of(koff, tq)
        s_all = _dot(qs_sc[...], kt_ref[0, :, pl.ds(koff, width)])
        pv = values(pl.multiple_of(jnp.maximum(koff - wide, 0), tq), wide, p_in)
        bias = (cref - c_ref[0, :, pl.ds(koff, width)]) * LOG2E
        nr = FOX_ROW_CHUNK
        if shift is not None:
            row = lax.broadcasted_iota(jnp.int32, (tq, tq), 0)
            col = lax.broadcasted_iota(jnp.int32, (tq, tq), 1)
            causal = col <= row
        for h in range(nh):
            hrows = slice(h * tq, (h + 1) * tq)
            sb = s_all[hrows] + bias[h:h + 1]
            if shift is not None:
                tail = jnp.where(causal, sb[:, shift:], NEG_INF)
                sb = tail if shift == 0 else jnp.concatenate([sb[:, :shift], tail], axis=1)
            m_old = m_sc[hrows, :]
            m_new = jnp.maximum(m_old, jnp.max(sb, axis=-1, keepdims=True))
            alpha = jnp.exp2(m_old - m_new)
            m_sc[hrows, :] = m_new
            for r0 in range(0, tq, nr):
                cr = slice(r0, r0 + nr)
                rows = slice(h * tq + r0, h * tq + r0 + nr)
                p_out[rows, 0:width] = jnp.exp2(
                    (sb[cr] - jnp.tile(m_new[cr], (1, width // LANES))).astype(BF16))
                acc_sc[rows, :] = alpha[cr] * (acc_sc[rows, :] + pv[h][cr])

    def finish(koff, width, p_in):
        pv = values(pl.multiple_of(koff, tq), width, p_in)
        acc = acc_sc[...] + jnp.concatenate(pv, axis=0)
        outs = []
        for g2 in range(nh // 2):
            even = acc[(2 * g2) * tq:(2 * g2 + 1) * tq]
            odd = acc[(2 * g2 + 1) * tq:(2 * g2 + 2) * tq]
            both = jnp.where(lane_head == 0, even, odd)
            sums = jnp.where(lane_head == 0, pltpu.roll(even, FOX_HEAD_DIM, 1),
                             pltpu.roll(odd, FOX_HEAD_DIM, 1))
            outs.append(both / sums)
        o_ref[0] = jnp.concatenate(outs, axis=1).astype(o_ref.dtype)

    nfull = qi // 2

    def pair(kk, carry):
        block(2 * kk * wide, wide, p1_sc, p0_sc, None)
        block((2 * kk + 1) * wide, wide, p0_sc, p1_sc, None)
        return carry

    lax.fori_loop(0, nfull // 2, pair, 0)

    for case in range(4):
        odd_full, odd_q = case // 2, case % 2
        twidth, tshift = (wide, tq) if odd_q else (tq, 0)

        @pl.when(qi % 4 == case)
        def _(odd_full=odd_full, twidth=twidth, tshift=tshift):
            toff = nfull * wide
            if odd_full:
                block(toff - wide, wide, p1_sc, p0_sc, None)
                block(toff, twidth, p0_sc, p1_sc, tshift)
                finish(toff, twidth, p1_sc)
            else:
                block(toff, twidth, p1_sc, p0_sc, tshift)
                finish(toff, twidth, p0_sc)


def _fox_prompt(q, kt, v, c, nb, t):
    tq = min(FOX_TQ, t // 2)
    assert t % (2 * tq) == 0 and tq % LANES == 0
    nslab = FOX_WIDTH // MXU_DIM
    rows = HEADS_PER_SLAB * tq
    return pl.pallas_call(
        functools.partial(_fox_kernel, tq=tq),
        grid=(nb, nslab, t // tq),
        in_specs=[
            pl.BlockSpec((1, tq, MXU_DIM), lambda b, g, i: (b, i, g)),
            pl.BlockSpec((1, MXU_DIM, t), lambda b, g, i: (b, g, 0)),
            pl.BlockSpec((1, t, MXU_DIM), lambda b, g, i: (b, 0, g)),
            pl.BlockSpec((1, HEADS_PER_SLAB, t), lambda b, g, i: (g, 0, b)),
        ],
        out_specs=pl.BlockSpec((1, tq, MXU_DIM), lambda b, g, i: (b, i, g)),
        out_shape=jax.ShapeDtypeStruct((nb, t, FOX_WIDTH), BF16),
        scratch_shapes=[pltpu.VMEM((rows, MXU_DIM), BF16),
                        pltpu.VMEM((rows, 2 * tq), BF16),
                        pltpu.VMEM((rows, 2 * tq), BF16),
                        pltpu.VMEM((rows, LANES), F32),
                        pltpu.VMEM((rows, LANES), F32)],
        compiler_params=_cparams(("parallel", "parallel", "arbitrary")),
        name="fox_prompt",
    )(q, kt, v, c)


def _fox_sample_kernel(pt_ref, q_ref, kn_ref, vn_ref, lfn_ref, kc_hbm, vc_hbm, lc_hbm,
                       o_ref, kbuf, vbuf, lbuf, sem, m_sc, l_sc, acc_sc, *, ppc, nch, page):
    b = pl.program_id(0)
    nb = pl.num_programs(0)
    nrow = q_ref.shape[1]
    nt = nrow // FOX_HEADS

    def copies(bb, c, slot):
        out = []
        for p in range(ppc):
            pid = pt_ref[bb, c * ppc + p]
            out.append(pltpu.make_async_copy(kc_hbm.at[pid], kbuf.at[slot, p], sem.at[0, slot]))
            out.append(pltpu.make_async_copy(vc_hbm.at[pid], vbuf.at[slot, p], sem.at[1, slot]))
            out.append(pltpu.make_async_copy(lc_hbm.at[pid], lbuf.at[slot, p], sem.at[2, slot]))
        return out

    def start(bb, c, slot):
        for cp in copies(bb, c, slot):
            cp.start()

    def wait(bb, c, slot):
        for cp in copies(bb, c, slot):
            cp.wait()

    @pl.when(b == 0)
    def _():
        start(0, 0, 0)

    q = q_ref[0]
    tri = _tri_upper(page)
    m_sc[...] = jnp.full(m_sc.shape, NEG_INF, F32)
    l_sc[...] = jnp.zeros(l_sc.shape, F32)
    acc_sc[...] = jnp.zeros(acc_sc.shape, F32)

    def update(s, pv_fn):
        m_old = m_sc[...]
        m_new = jnp.maximum(m_old, jnp.max(s, axis=-1, keepdims=True))
        alpha = jnp.exp2(m_old - m_new)
        p = jnp.exp2(s - m_new)
        l_sc[...] = alpha * l_sc[...] + jnp.sum(p, axis=-1, keepdims=True)
        acc_sc[...] = alpha * acc_sc[...] + pv_fn(p)
        m_sc[...] = m_new

    carry = jnp.zeros((FOX_HEADS, 1), F32)
    for c in range(nch):
        slot = (b * nch + c) % 2
        if c + 1 < nch:
            start(b, c + 1, 1 - slot)
        else:
            @pl.when(b + 1 < nb)
            def _():
                start(b + 1, 0, 1 - slot)
        wait(b, c, slot)
        w = _cumsum_lanes(lbuf[slot].reshape(ppc * FOX_HEADS, page), tri)
        cs, ss = [], []
        for p in range(ppc):
            wp = w[p * FOX_HEADS:(p + 1) * FOX_HEADS]
            cs.append(wp + carry)
            carry = carry + wp[:, page - 1:page]
            ss.append(_dot(q, kbuf[slot, p]))
        ck = jnp.concatenate(cs, axis=1) * LOG2E
        s = jnp.concatenate(ss, axis=1) - jnp.concatenate([ck] * nt, axis=0)

        def pv_pages(pr, slot=slot):
            out = _dot_nt(pr[:, 0:page], vbuf[slot, 0])
            for p in range(1, ppc):
                out += _dot_nt(pr[:, p * page:(p + 1) * page], vbuf[slot, p])
            return out

        update(s, pv_pages)

    lfn = lfn_ref[0]
    cn = (carry + _cumsum_lanes(lfn, _tri_upper(LANES))) * LOG2E
    sn = _dot_nt(q, kn_ref[0]) - jnp.concatenate([cn[:, 0:SUBLANES]] * nt, axis=0)
    r = lax.broadcasted_iota(jnp.int32, (nrow, SUBLANES), 0) // FOX_HEADS
    j = lax.broadcasted_iota(jnp.int32, (nrow, SUBLANES), 1)
    sn = jnp.where(j <= r, sn, NEG_INF)
    update(sn, lambda pr: _dot(pr, vn_ref[0]))

    res = acc_sc[...] / l_sc[...]
    lane_head = lax.broadcasted_iota(jnp.int32, (FOX_HEADS, FOX_WIDTH), 1) // FOX_HEAD_DIM
    sub_head = lax.broadcasted_iota(jnp.int32, (FOX_HEADS, FOX_WIDTH), 0)
    rows = []
    for t in range(nt):
        blk = res[t * FOX_HEADS:(t + 1) * FOX_HEADS]
        rows.append(jnp.sum(jnp.where(lane_head == sub_head, blk, 0.0), axis=0, keepdims=True))
    o_ref[0] = jnp.concatenate(rows, axis=0).astype(o_ref.dtype)


def _fox_sample(q, k_new, v_new, lft_new, cache_k, cache_v, cache_lf, page_table):
    nb, nt, _ = q.shape
    n_pool, page = cache_k.shape[0], cache_k.shape[1]
    n_pages = page_table.shape[1]
    ppc = min(FOX_PAGES_PER_CHUNK, n_pages)
    nch = n_pages // ppc
    nrow = nt * FOX_HEADS
    qf = q.astype(F32)
    head_of_lane = jnp.arange(FOX_WIDTH) // FOX_HEAD_DIM
    qrows = jnp.where(head_of_lane[None, None, None, :] == jnp.arange(FOX_HEADS)[None, None, :, None],
                      qf[:, :, None, :], 0.0).reshape(nb, nrow, FOX_WIDTH)
    pad = SUBLANES - nt
    kn = jnp.pad(k_new, ((0, 0), (0, pad), (0, 0)))
    vn = jnp.pad(v_new, ((0, 0), (0, pad), (0, 0)))
    lfn = jnp.pad(lft_new.reshape(FOX_HEADS, nb, nt).transpose(1, 0, 2),
                  ((0, 0), (0, 0), (0, LANES - nt)))
    kc2 = cache_k.transpose(0, 2, 3, 1).reshape(n_pool, FOX_WIDTH, page)
    vc2 = cache_v.transpose(0, 2, 3, 1).reshape(n_pool, FOX_WIDTH, page)
    lc2 = cache_lf.transpose(0, 2, 1)
    per_b = lambda shape: pl.BlockSpec((1,) + shape, lambda b, pt: (b, 0, 0))
    grid_spec = pltpu.PrefetchScalarGridSpec(
        num_scalar_prefetch=1,
        grid=(nb,),
        in_specs=[per_b((nrow, FOX_WIDTH)), per_b((SUBLANES, FOX_WIDTH)), per_b((SUBLANES, FOX_WIDTH)),
                  per_b((FOX_HEADS, LANES)),
                  pl.BlockSpec(memory_space=pl.ANY), pl.BlockSpec(memory_space=pl.ANY),
                  pl.BlockSpec(memory_space=pl.ANY)],
        out_specs=per_b((nt, FOX_WIDTH)),
        scratch_shapes=[
            pltpu.VMEM((2, ppc, FOX_WIDTH, page), F32),
            pltpu.VMEM((2, ppc, FOX_WIDTH, page), F32),
            pltpu.VMEM((2, ppc, FOX_HEADS, page), F32),
            pltpu.SemaphoreType.DMA((3, 2)),
            pltpu.VMEM((nrow, 1), F32), pltpu.VMEM((nrow, 1), F32),
            pltpu.VMEM((nrow, FOX_WIDTH), F32),
        ],
    )
    return pl.pallas_call(
        functools.partial(_fox_sample_kernel, ppc=ppc, nch=nch, page=page),
        grid_spec=grid_spec,
        out_shape=jax.ShapeDtypeStruct((nb, nt, FOX_WIDTH), BF16),
        compiler_params=_cparams(("arbitrary",)),
        name="fox_sample",
    )(page_table, qrows, kn, vn, lfn, kc2, vc2, lc2)


def _s5_kernel(u_ref, h0_ref, ar_ref, ai_ref, bm_ref, cm_ref, d_ref, wg_ref, bg_ref,
               o_ref, hf_ref, s_sc, h_sc, *, tt, nslab):
    ti = pl.program_id(1)
    ns = S5_NSTATE
    half = MXU_DIM
    hs = ns // 2

    @pl.when(ti == 0)
    def _():
        h_sc[...] = h0_ref[...]

    u = u_ref[...]
    ub = u.astype(BF16)
    for h in range(2):
        uh = ub[:, h * half:(h + 1) * half]
        s_sc[:, h * hs:(h + 1) * hs] = _dot(uh, bm_ref[2 * h])
        s_sc[:, ns + h * hs:ns + (h + 1) * hs] = _dot(uh, bm_ref[2 * h + 1])

    sw = ns // nslab
    for sl in range(nslab):
        lo = sl * sw
        ar = jnp.broadcast_to(ar_ref[:, lo:lo + sw], (SUBLANES, sw))
        ai = jnp.broadcast_to(ai_ref[:, lo:lo + sw], (SUBLANES, sw))

        def body(t, carry):
            hr, hi = carry
            r0 = pl.multiple_of(t * SUBLANES, SUBLANES)
            nr = ar * hr - ai * hi + s_sc[pl.ds(r0, SUBLANES), lo:lo + sw]
            ni = ar * hi + ai * hr + s_sc[pl.ds(r0, SUBLANES), ns + lo:ns + lo + sw]
            s_sc[pl.ds(r0, SUBLANES), lo:lo + sw] = nr
            s_sc[pl.ds(r0, SUBLANES), ns + lo:ns + lo + sw] = ni
            return nr, ni

        hr, hi = lax.fori_loop(0, tt, body, (h_sc[:, lo:lo + sw], h_sc[:, ns + lo:ns + lo + sw]))
        h_sc[:, lo:lo + sw] = hr
        h_sc[:, ns + lo:ns + lo + sw] = hi

    ys = []
    for h in range(2):
        yr = _dot(s_sc[:, h * hs:(h + 1) * hs].astype(BF16), cm_ref[2 * h])
        yi = _dot(s_sc[:, ns + h * hs:ns + (h + 1) * hs].astype(BF16), cm_ref[2 * h + 1])
        ys.append(yr + yi)
    y = jnp.concatenate(ys, axis=1) + d_ref[...] * u
    zg = 0.5 * y * (1.0 + jnp.tanh(math.sqrt(2.0 / math.pi) * (y + 0.044715 * (y * y * y))))
    gl = _dot(zg.astype(BF16), wg_ref[...]) + bg_ref[...]
    o_ref[...] = (zg * _sigmoid(gl)).astype(o_ref.dtype)

    @pl.when(ti == pl.num_programs(1) - 1)
    def _():
        hf_ref[...] = h_sc[...]


def _s5(u_tb, h0, ar, ai, bmat, cmat, d, wglu, bglu, nbg, t):
    tt = min(S5_TT, t)
    nti = t // tt
    rows = tt * SUBLANES
    full = lambda a: pl.BlockSpec(a.shape, lambda g, i: (0,) * a.ndim)
    tok = pl.BlockSpec((rows, S5_WIDTH), lambda g, i: (g * nti + i, 0))
    st = pl.BlockSpec((SUBLANES, 2 * S5_NSTATE), lambda g, i: (g, 0))
    return pl.pallas_call(
        functools.partial(_s5_kernel, tt=tt, nslab=4),
        grid=(nbg, nti),
        in_specs=[tok, st, full(ar), full(ai), full(bmat), full(cmat), full(d), full(wglu), full(bglu)],
        out_specs=(tok, st),
        out_shape=(jax.ShapeDtypeStruct(u_tb.shape, BF16),
                   jax.ShapeDtypeStruct(h0.shape, F32)),
        scratch_shapes=[pltpu.VMEM((rows, 2 * S5_NSTATE), F32),
                        pltpu.VMEM((SUBLANES, 2 * S5_NSTATE), F32)],
        compiler_params=_cparams(("parallel", "arbitrary")),
        name="s5",
    )(u_tb, h0, ar, ai, bmat, cmat, d, wglu, bglu)


def _memkv_kernel(m_ref, g_ref, w_ref, k32_ref, v32_ref, kb_ref, vb_ref):
    xn = _rms(m_ref[...], g_ref[...]).astype(BF16)
    k = _dot(xn, w_ref[:, 0:MEM_WIDTH])
    v = _dot(xn, w_ref[:, MEM_WIDTH:2 * MEM_WIDTH])
    k32_ref[...] = k
    v32_ref[...] = v
    kb_ref[...] = k.astype(BF16)
    vb_ref[...] = v.astype(BF16)


def _memkv(mem2d, g_mem, w_kv):
    n, d = mem2d.shape
    tm = min(PROJ_TM, n)
    g2 = g_mem.reshape(1, d)
    tok = lambda width: pl.BlockSpec((tm, width), lambda i: (i, 0))
    full = lambda a: pl.BlockSpec(a.shape, lambda i: (0,) * a.ndim)
    sh = lambda dt: jax.ShapeDtypeStruct((n, MEM_WIDTH), dt)
    return pl.pallas_call(
        _memkv_kernel,
        grid=(n // tm,),
        in_specs=[tok(d), full(g2), full(w_kv)],
        out_specs=(tok(MEM_WIDTH),) * 4,
        out_shape=(sh(F32), sh(F32), sh(BF16), sh(BF16)),
        compiler_params=_cparams(("parallel",)),
        name="mem_kv",
    )(mem2d, g2, w_kv)


def _memattn_kernel(q_ref, k_ref, v_ref, o_ref, *, head_rows):
    q = q_ref[0]
    outs = []
    for h in range(MEM_HEADS):
        sl = slice(h * MEM_HEAD_DIM, (h + 1) * MEM_HEAD_DIM)
        if head_rows:
            n_mem = k_ref.shape[1] // MEM_HEADS
            kh = k_ref[0, pl.ds(h, n_mem, stride=MEM_HEADS), :].astype(BF16)
            vh = v_ref[0, pl.ds(h, n_mem, stride=MEM_HEADS), :].astype(BF16)
        else:
            kh = k_ref[0, :, sl].astype(BF16)
            vh = v_ref[0, :, sl].astype(BF16)
        s = _dot_nt(q[:, sl], kh)
        p = jnp.exp(s - jnp.max(s, axis=-1, keepdims=True))
        o = _dot(p.astype(BF16), vh)
        outs.append(o / jnp.sum(p, axis=-1, keepdims=True))
    o_ref[0] = jnp.concatenate(outs, axis=1).astype(o_ref.dtype)


def _memattn(qm, mk, mv):
    nb, t, _ = qm.shape
    tq = min(MEMATTN_TQ, t)
    kv_spec = pl.BlockSpec((1,) + mk.shape[1:], lambda b, i: (b, 0, 0))
    return pl.pallas_call(
        functools.partial(_memattn_kernel, head_rows=mk.shape[2] == MEM_HEAD_DIM),
        grid=(nb, t // tq),
        in_specs=[pl.BlockSpec((1, tq, MEM_WIDTH), lambda b, i: (b, i, 0)), kv_spec, kv_spec],
        out_specs=pl.BlockSpec((1, tq, MEM_WIDTH), lambda b, i: (b, i, 0)),
        out_shape=jax.ShapeDtypeStruct(qm.shape, BF16),
        compiler_params=_cparams(("parallel", "parallel")),
        name="mem_attn",
    )(qm, mk, mv)


def _merge_kernel(x_ref, g_ref, wg_ref, of_ref, os_ref, om_ref, wf_ref, ws_ref, wm_ref, wo_ref, o_ref):
    x = x_ref[...]
    d = x.shape[1]
    xn = _rms(x, g_ref[...]).astype(BF16)
    merged = _sigmoid(_dot(xn, wg_ref[:, 0:d])) * _dot(of_ref[...], wf_ref[...])
    merged += _sigmoid(_dot(xn, wg_ref[:, d:2 * d])) * _dot(os_ref[...], ws_ref[...])
    merged += _sigmoid(_dot(xn, wg_ref[:, 2 * d:3 * d])) * _dot(om_ref[...], wm_ref[...])
    o_ref[...] = x + _dot(merged.astype(BF16), wo_ref[...])


def _merge(x2d, g_mix, w_gate, o_fox, o_s5, o_mem, w_f, w_s, w_m, w_out):
    n, d = x2d.shape
    tm = min(MERGE_TM, n)
    g2 = g_mix.reshape(1, d)
    tok = lambda width: pl.BlockSpec((tm, width), lambda i: (i, 0))
    full = lambda a: pl.BlockSpec(a.shape, lambda i: (0,) * a.ndim)
    return pl.pallas_call(
        _merge_kernel,
        grid=(n // tm,),
        in_specs=[tok(d), full(g2), full(w_gate), tok(FOX_WIDTH), tok(S5_WIDTH), tok(MEM_WIDTH),
                  full(w_f), full(w_s), full(w_m), full(w_out)],
        out_specs=tok(d),
        out_shape=jax.ShapeDtypeStruct((n, d), F32),
        compiler_params=_cparams(("parallel",)),
        name="merge",
    )(x2d, g2, w_gate, o_fox, o_s5, o_mem, w_f, w_s, w_m, w_out)


def _first_argmax(vals):
    m = vals[0]
    for v in vals[1:]:
        m = jnp.maximum(m, v)
    idx = jnp.full(m.shape, len(vals) - 1, jnp.int32)
    for i in range(len(vals) - 2, -1, -1):
        idx = jnp.where(vals[i] == m, i, idx)
    return idx, m


def _moe_kernel(x_ref, g_ref, wr_ref, wrl_ref, br_ref, wg_ref, wu_ref, wd_ref, gf_ref, o_ref,
                h_sc, r_sc, acc_sc):
    j = pl.program_id(1)
    ng, ne = N_EXPERT_GROUPS, EXPERTS_PER_GROUP

    @pl.when(j == 0)
    def _():
        hf = _rms(x_ref[...], g_ref[...])
        h = hf.astype(BF16)
        h_sc[...] = h
        h_lo = (hf - h.astype(F32)).astype(BF16)
        lg = _dot(h, wr_ref[...]) + _dot(h_lo, wr_ref[...]) + _dot(h, wrl_ref[...]) + br_ref[...]
        lt = lg.T
        col = lambda i: lt[i:i + 1, :]
        gl = [col(i) for i in range(ng)]
        gidx, gm = _first_argmax(gl)
        den = jnp.exp(gl[0] - gm)
        for v in gl[1:]:
            den = den + jnp.exp(v - gm)
        p_sel = 1.0 / den
        le = []
        for e in range(ne):
            v = col(ng + e)
            for g in range(1, ng):
                v = jnp.where(gidx == g, col(ng + g * ne + e), v)
            le.append(v)
        i1, v1 = _first_argmax(le)
        rest = [jnp.where(i1 == e, -jnp.inf, le[e]) for e in range(ne)]
        i2, v2 = _first_argmax(rest)
        t = jnp.exp(v2 - v1)
        w1 = 1.0 / (1.0 + t)
        w2 = t / (1.0 + t)
        rows = [jnp.where(i1 == e, w1, 0.0) + jnp.where(i2 == e, w2, 0.0) for e in range(ne)]
        rows += [p_sel, gidx.astype(F32)]
        rt = jnp.concatenate(rows + [jnp.zeros((LANES - len(rows), lt.shape[1]), F32)], axis=0)
        r_sc[...] = rt.T
        acc_sc[...] = jnp.zeros(acc_sc.shape, F32)

    h = h_sc[...]
    r = r_sc[...]
    scale = jnp.where(r[:, ne + 1:ne + 2] == j.astype(F32), r[:, ne:ne + 1], 0.0)
    down = None
    for e in range(ne):
        sl = slice(e * D_EXPERT, (e + 1) * D_EXPERT)
        hg = _dot(h, wg_ref[:, sl])
        hu = _dot(h, wu_ref[:, sl])
        act = (hg * _sigmoid(hg) * hu * (scale * r[:, e:e + 1])).astype(BF16)
        d = _dot(act, wd_ref[sl, :])
        down = d if down is None else down + d
    acc_sc[...] += down

    @pl.when(j == ng - 1)
    def _():
        o_ref[...] = _rms(x_ref[...] + acc_sc[...], gf_ref[...])


def _moe(x2d, g_ffn, w_r, w_r_lo, b_r, w_g, w_u, w_d, g_final):
    n, d = x2d.shape
    tm = min(MOE_TM, n)
    gw = EXPERTS_PER_GROUP * D_EXPERT
    g2 = g_ffn.reshape(1, d)
    gf2 = g_final.reshape(1, d)
    tok = pl.BlockSpec((tm, d), lambda i, j: (i, 0))
    full = lambda a: pl.BlockSpec(a.shape, lambda i, j: (0,) * a.ndim)
    return pl.pallas_call(
        _moe_kernel,
        grid=(n // tm, N_EXPERT_GROUPS),
        in_specs=[tok, full(g2), full(w_r), full(w_r_lo), full(b_r),
                  pl.BlockSpec((d, gw), lambda i, j: (0, j)),
                  pl.BlockSpec((d, gw), lambda i, j: (0, j)),
                  pl.BlockSpec((gw, d), lambda i, j: (j, 0)),
                  full(gf2)],
        out_specs=tok,
        out_shape=jax.ShapeDtypeStruct((n, d), F32),
        scratch_shapes=[pltpu.VMEM((tm, d), BF16), pltpu.VMEM((tm, LANES), F32),
                        pltpu.VMEM((tm, d), F32)],
        compiler_params=_cparams(("parallel", "arbitrary")),
        name="moe",
    )(x2d, g2, w_r, w_r_lo, b_r, w_g, w_u, w_d, gf2)


def _block_diag(blocks):
    nblk, r, c = blocks.shape
    eye = jnp.eye(nblk, dtype=blocks.dtype)
    return (eye[:, None, :, None] * blocks[:, :, None, :]).reshape(nblk * r, nblk * c)


def _prep_params(g_mix, w_in, b_forget, s5_a_re, s5_a_im, s5_log_dt, s5_b_re, s5_b_im, s5_c_re, s5_c_im,
                 s5_d, s5_w_glu, s5_b_glu, w_br_fox, w_br_s5, w_br_mem, w_out,
                 w_router_group, b_router_group, w_router_expert, b_router_expert,
                 w_exp_gate, w_exp_up, w_exp_down):
    d = w_in.shape[0]
    o3 = 3 * FOX_WIDTH
    o4 = o3 + FOX_HEADS
    o6 = o4 + S5_WIDTH + MEM_WIDTH
    p = {}
    o1, o2 = FOX_WIDTH, 2 * FOX_WIDTH
    p['w_a'] = jnp.concatenate([w_in[:, :o1], w_in[:, o2:o3], w_in[:, o4:o6]], axis=1).astype(BF16)
    p['w_t'] = w_in[:, o1:o3].T.astype(BF16)
    wft = jnp.zeros((2 * SUBLANES, d), F32).at[:FOX_HEADS].set(w_in[:, o3:o4].T)
    p['wft'] = wft.astype(BF16)
    p['w_gate'] = w_in[:, o6:].astype(BF16)
    a_re = jnp.minimum(s5_a_re.astype(F32), A_RE_MAX)
    a_im = s5_a_im.astype(F32)
    dt = jnp.exp(s5_log_dt.astype(F32))[:, None]
    mag = jnp.exp(dt * a_re)
    ang = dt * a_im
    ab_re = mag * jnp.cos(ang)
    ab_im = mag * jnp.sin(ang)
    den = a_re * a_re + a_im * a_im
    n_re = ab_re - 1.0
    n_im = ab_im
    s_re = (n_re * a_re + n_im * a_im) / den
    s_im = (n_im * a_re - n_re * a_im) / den
    bb_re = s_re[..., None] * s5_b_re - s_im[..., None] * s5_b_im
    bb_im = s_re[..., None] * s5_b_im + s_im[..., None] * s5_b_re
    p['ar'] = ab_re.reshape(1, S5_NSTATE)
    p['ai'] = ab_im.reshape(1, S5_NSTATE)
    gh = S5_GROUPS // 2
    bm, cm = [], []
    for h in range(2):
        gs = slice(h * gh, (h + 1) * gh)
        bm.append(_block_diag(bb_re[gs].transpose(0, 2, 1)))
        bm.append(_block_diag(bb_im[gs].transpose(0, 2, 1)))
        cm.append(_block_diag(s5_c_re[gs].transpose(0, 2, 1)))
        cm.append(_block_diag(-s5_c_im[gs].transpose(0, 2, 1)))
    p['bmat'] = jnp.stack(bm).astype(BF16)
    p['cmat'] = jnp.stack(cm).astype(BF16)
    p['d'] = s5_d.reshape(1, S5_WIDTH).astype(F32)
    p['wglu'] = s5_w_glu.astype(BF16)
    p['bglu'] = s5_b_glu.reshape(1, S5_WIDTH).astype(F32)
    p['w_f'] = w_br_fox.astype(BF16)
    p['w_s'] = w_br_s5.astype(BF16)
    p['w_m'] = w_br_mem.astype(BF16)
    p['w_out'] = w_out.astype(BF16)
    nr = N_EXPERT_GROUPS * (1 + EXPERTS_PER_GROUP)
    w_r = jnp.concatenate([w_router_group, w_router_expert.reshape(d, -1)], axis=1)
    w_r = jnp.pad(w_r, ((0, 0), (0, LANES - nr))).astype(F32)
    p['w_r'] = w_r.astype(BF16)
    p['w_r_lo'] = (w_r - p['w_r'].astype(F32)).astype(BF16)
    b_r = jnp.concatenate([b_router_group, b_router_expert.reshape(-1)])
    p['b_r'] = jnp.pad(b_r, (0, LANES - nr)).reshape(1, LANES).astype(F32)
    p['w_eg'] = w_exp_gate.astype(BF16)
    p['w_eu'] = w_exp_up.astype(BF16)
    p['w_ed'] = w_exp_down.astype(BF16)
    p['g_mix'] = g_mix
    p['b_forget'] = b_forget
    return p


def _to_time_major(a, nbg, t):
    w = a.shape[-1]
    return a.reshape(nbg, SUBLANES, t, w).transpose(0, 2, 1, 3).reshape(nbg * t * SUBLANES, w)


def _from_time_major(a, nbg, t):
    w = a.shape[-1]
    return a.reshape(nbg, t, SUBLANES, w).transpose(0, 2, 1, 3).reshape(nbg * SUBLANES, t, w)


def _layer(x, p, fox_fn, h0_re, h0_im, mem_k, mem_v, g_ffn, g_final):
    nb, t, d = x.shape
    n = nb * t
    x2d = x.reshape(n, d)
    pnb, pt = (nb, t) if t % LANES == 0 else (1, n)
    q, kt32, ktb, vt32, vb, u, qm, lft = _proj(x2d, pnb, pt, p['g_mix'], p['w_a'], p['w_t'], p['wft'],
                                               p['b_forget'])
    r3 = lambda a: a.reshape(nb, t, a.shape[-1])
    heads_last = lambda a: (a.reshape(pnb, FOX_HEADS, FOX_HEAD_DIM, pt).transpose(0, 3, 1, 2)
                            .reshape(nb, t, FOX_HEADS, FOX_HEAD_DIM))
    k_out = heads_last(kt32)
    v_out = heads_last(vt32)
    o_fox = fox_fn(r3(q), ktb, r3(vb), k_out.reshape(nb, t, FOX_WIDTH), v_out.reshape(nb, t, FOX_WIDTH), lft)
    nbg = nb // SUBLANES
    h0 = jnp.concatenate([h0_re.reshape(nb, S5_NSTATE), h0_im.reshape(nb, S5_NSTATE)], axis=1).astype(F32)
    o_s5_tb, hf = _s5(_to_time_major(r3(u), nbg, t), h0, p['ar'], p['ai'], p['bmat'], p['cmat'],
                      p['d'], p['wglu'], p['bglu'], nbg, t)
    o_s5 = _from_time_major(o_s5_tb, nbg, t)
    o_mem = _memattn(r3(qm), mem_k, mem_v)
    x1 = _merge(x2d, p['g_mix'], p['w_gate'], o_fox.reshape(n, FOX_WIDTH), o_s5.reshape(n, S5_WIDTH),
                o_mem.reshape(n, MEM_WIDTH), p['w_f'], p['w_s'], p['w_m'], p['w_out'])
    y = _moe(x1, g_ffn, p['w_r'], p['w_r_lo'], p['b_r'], p['w_eg'], p['w_eu'], p['w_ed'], g_final)
    lf_out = lft.reshape(FOX_HEADS, nb, t).transpose(1, 2, 0)
    h_re = hf[:, :S5_NSTATE].reshape(nb, S5_GROUPS, S5_STATE)
    h_im = hf[:, S5_NSTATE:].reshape(nb, S5_GROUPS, S5_STATE)
    return y.reshape(nb, t, d), k_out, v_out, lf_out, h_re, h_im


def kernel(x_prompt, x_sample, cache_fox_k, cache_fox_v, cache_fox_logf, cache_mem_k, cache_mem_v, state_s5_re, state_s5_im, page_table, mem_prompt, g_mix, w_in, b_forget, s5_a_re, s5_a_im, s5_log_dt, s5_b_re, s5_b_im, s5_c_re, s5_c_im, s5_d, s5_w_glu, s5_b_glu, g_mem, w_mem_kv, w_br_fox, w_br_s5, w_br_mem, w_out, g_ffn, w_router_group, b_router_group, w_router_expert, b_router_expert, w_exp_gate, w_exp_up, w_exp_down, g_final):
    depth = w_in.shape[0]
    assert depth == 1, "single-layer step"
    l = 0
    bp, sp, d = x_prompt.shape
    db, ds, _ = x_sample.shape
    n_mem = mem_prompt.shape[1]
    p = _prep_params(g_mix[l], w_in[l], b_forget[l], s5_a_re[l], s5_a_im[l], s5_log_dt[l], s5_b_re[l],
                     s5_b_im[l], s5_c_re[l], s5_c_im[l], s5_d[l], s5_w_glu[l], s5_b_glu[l],
                     w_br_fox[l], w_br_s5[l], w_br_mem[l], w_out[l],
                     w_router_group[l], b_router_group[l], w_router_expert[l], b_router_expert[l],
                     w_exp_gate[l], w_exp_up[l], w_exp_down[l])

    mk32, mv32, mkb, mvb = _memkv(mem_prompt.reshape(bp * n_mem, d), g_mem[l], w_mem_kv[l].astype(BF16))
    mem3 = lambda a: a.reshape(bp, n_mem, MEM_WIDTH)

    def fox_p(q, ktb, vb, k32, v32, lft):
        c = _cumsum(lft, bp, sp)
        c = c.reshape(FOX_WIDTH // MXU_DIM, HEADS_PER_SLAB, bp * sp)
        return _fox_prompt(q, ktb, vb, c, bp, sp)

    zero = jnp.zeros((bp, S5_GROUPS, S5_STATE), F32)
    yp, kp, vp, lfp, hr_p, hi_p = _layer(x_prompt, p, fox_p, zero, zero, mem3(mkb), mem3(mvb),
                                         g_ffn[l], g_final)

    def fox_s(q, ktb, vb, k32, v32, lft):
        return _fox_sample(q, k32, v32, lft, cache_fox_k[l], cache_fox_v[l], cache_fox_logf[l], page_table)

    ys, ks, vs, lfs, hr_s, hi_s = _layer(x_sample, p, fox_s, state_s5_re[l], state_s5_im[l],
                                         cache_mem_k[l].reshape(db, n_mem * MEM_HEADS, MEM_HEAD_DIM),
                                         cache_mem_v[l].reshape(db, n_mem * MEM_HEADS, MEM_HEAD_DIM),
                                         g_ffn[l], g_final)
    e = lambda a: a[None]
    mk4 = mk32.reshape(bp, n_mem, MEM_HEADS, MEM_HEAD_DIM)
    mv4 = mv32.reshape(bp, n_mem, MEM_HEADS, MEM_HEAD_DIM)
    return (yp, ys, e(kp), e(vp), e(lfp), e(ks), e(vs), e(lfs), e(mk4), e(mv4),
            e(hr_p), e(hi_p), e(hr_s), e(hi_s))
```
